```python
import math
import jax, jax.numpy as jnp
from jax import lax
import numpy as np

D_MODEL = 1024
BATCH = 8
SEQ = 4096
DEPTH = 1
DEC_BATCH = 128
DEC_SEQ = 4
PAST_LEN = 8192
PAGE_SIZE = 128

HEAD_DIM = 64
N_HEADS_ATTN = 8
N_HEADS_RWKV = 8
C_ATTN = N_HEADS_ATTN * HEAD_DIM
C_RWKV = N_HEADS_RWKV * HEAD_DIM
DILATIONS = ((128, 1), (512, 4), (2048, 16))
MAX_WINDOW = 2048
N_BUCKETS = 32
MAX_DISTANCE = 2048
LORA_DECAY = 32
LORA_ICLR = 32
LORA_GATE = 64
COLS_RWKV = 3 * C_RWKV + LORA_DECAY + LORA_ICLR + LORA_GATE
D_IN = 3 * C_ATTN + COLS_RWKV
PEER_HEADS = 8
PEER_KEYS = 128
PEER_EXPERTS = PEER_KEYS * PEER_KEYS
PEER_QDIM = 256
PEER_HALF = PEER_QDIM // 2
PEER_TOPK = 16
PEER_CHUNK = 256
NORM_EPS = 1e-6
GN_EPS = 64e-5
NEG_INF = -1e30
ATTN_SCALE = HEAD_DIM ** -0.5

kernel_name = 'hymba_dilated_rwkv7_peer_step'


def _rms(x, g):
    x32 = x.astype(jnp.float32)
    return x32 * lax.rsqrt(jnp.mean(x32 * x32, axis=-1, keepdims=True) + NORM_EPS) * g


def _t5_bucket(dist):
    dist = np.asarray(dist, dtype=np.int64)
    max_exact = N_BUCKETS // 2
    safe = np.maximum(dist, 1) / max_exact
    large = max_exact + (np.log(safe) / math.log(MAX_DISTANCE / max_exact) * (N_BUCKETS - max_exact)).astype(np.int64)
    large = np.minimum(large, N_BUCKETS - 1)
    return np.where(dist < max_exact, dist, large).astype(np.int32)


def _branch_prompt(q, k, v, rel_bias, window, dil):
    B, T, H, C = q.shape
    n = window // dil
    L = T + (-T) % window
    G = L // window

    def blocks(a):
        a = jnp.pad(a, ((0, 0), (0, L - T), (0, 0), (0, 0)))
        a = a.reshape(B, L // dil, dil, H, C).transpose(0, 2, 1, 3, 4)
        return a.reshape(B, dil, G, n, H, C)

    def with_prev(a):
        prev = jnp.pad(a[:, :, :-1], ((0, 0), (0, 0), (1, 0), (0, 0), (0, 0), (0, 0)))
        return jnp.concatenate([prev, a], axis=3)

    qb = blocks(q)
    kb = with_prev(blocks(k))
    vb = with_prev(blocks(v))
    qi = np.arange(n)[:, None]
    ki = np.arange(2 * n)[None, :]
    j = n + qi - ki
    band = (j >= 0) & (j <= n)
    pad_key = (np.arange(G) == 0)[:, None, None] & (ki < n)[None]
    mask = band[None] & ~pad_key
    bias = rel_bias[_t5_bucket(np.clip(j, 0, n) * dil)].transpose(2, 0, 1).astype(jnp.float32)
    logits = jnp.einsum('brgqhc,brgshc->brghqs', qb, kb).astype(jnp.float32) * ATTN_SCALE + bias
    logits = jnp.where(mask[None, None, :, None], logits, NEG_INF)
    m = jnp.max(logits, axis=-1, keepdims=True)
    e = jnp.exp(logits - m)
    s = jnp.sum(e, axis=-1)
    o = jnp.einsum('brghqs,brgshc->brgqhc', e, vb.astype(jnp.float32)) / s.transpose(0, 1, 2, 4, 3)[..., None]
    lse = (m[..., 0] + jnp.log(s)).transpose(0, 1, 2, 4, 3)
    o = o.reshape(B, dil, L // dil, H, C).transpose(0, 2, 1, 3, 4).reshape(B, L, H, C)[:, :T]
    lse = lse.reshape(B, dil, L // dil, H).transpose(0, 2, 1, 3).reshape(B, L, H)[:, :T]
    return o, lse


def _branch_sample(q, k_all, v_all, rel_bias, window, dil, lb):
    S = q.shape[1]
    n = window // dil
    jj = np.arange(n + 1)
    idx = lb + np.arange(S)[:, None] - jj[None, :] * dil
    valid = idx >= 0
    idx_c = np.maximum(idx, 0)
    ks = k_all[:, idx_c].astype(jnp.float32)
    vs = v_all[:, idx_c].astype(jnp.float32)
    bias = rel_bias[_t5_bucket(jj * dil)].T.astype(jnp.float32)
    logits = jnp.einsum('bqhc,bqjhc->bhqj', q, ks).astype(jnp.float32) * ATTN_SCALE + bias[None, :, None, :]
    logits = jnp.where(valid[None, None], logits, NEG_INF)
    m = jnp.max(logits, axis=-1, keepdims=True)
    e = jnp.exp(logits - m)
    s = jnp.sum(e, axis=-1)
    o = jnp.einsum('bhqj,bqjhc->bqhc', e, vs) / s.transpose(0, 2, 1)[..., None]
    lse = (m[..., 0] + jnp.log(s)).transpose(0, 2, 1)
    return o, lse


def _merge_branches(outs, lses):
    wts = jax.nn.softmax(jnp.stack(lses), axis=0)
    return jnp.einsum('ibth,ibthc->bthc', wts, jnp.stack(outs))


def _dilated_attn_prompt(q, k, v, rel_bias):
    outs, lses = [], []
    for window, dil in DILATIONS:
        o, lse = _branch_prompt(q, k, v, rel_bias, window, dil)
        outs.append(o)
        lses.append(lse)
    return _merge_branches(outs, lses)


def _dilated_attn_sample(q, k_new, v_new, k_buf, v_buf, rel_bias):
    lb = k_buf.shape[1]
    k_all = jnp.concatenate([k_buf.astype(jnp.float32), k_new.astype(jnp.float32)], axis=1)
    v_all = jnp.concatenate([v_buf.astype(jnp.float32), v_new.astype(jnp.float32)], axis=1)
    outs, lses = [], []
    for window, dil in DILATIONS:
        o, lse = _branch_sample(q, k_all, v_all, rel_bias, window, dil, lb)
        outs.append(o)
        lses.append(lse)
    return _merge_branches(outs, lses)


def _wkv_scan(r, w, k, v, kk, a, s0):
    def step(S, inp):
        r_t, w_t, k_t, v_t, kk_t, a_t = inp
        sa = jnp.einsum('bhvk,bhk->bhv', S, -kk_t)
        S = S * w_t[:, :, None, :] + sa[..., None] * (kk_t * a_t)[:, :, None, :] + v_t[..., None] * k_t[:, :, None, :]
        return S, jnp.einsum('bhvk,bhk->bhv', S, r_t)
    xs = tuple(jnp.moveaxis(t, 1, 0) for t in (r, w, k, v, kk, a))
    s_fin, y = lax.scan(step, s0, xs)
    return jnp.moveaxis(y, 0, 1), s_fin


def _rwkv_mixer(pb, shift0, s0, mu, w0, w_w2, a0, w_a2, w_g2, k_k, k_a, r_k, lnx_g, lnx_b):
    B, T, _ = pb.shape
    pb = pb.astype(jnp.float32)
    prev = jnp.concatenate([shift0[:, None, :].astype(jnp.float32), pb[:, :-1]], axis=1)
    xm = pb + (prev - pb) * mu
    c = C_RWKV
    r, k, v = xm[..., :c], xm[..., c:2 * c], xm[..., 2 * c:3 * c]
    o = 3 * c
    wl = xm[..., o:o + LORA_DECAY]
    al = xm[..., o + LORA_DECAY:o + LORA_DECAY + LORA_ICLR]
    gl = xm[..., o + LORA_DECAY + LORA_ICLR:]
    w_log = -jax.nn.softplus(-(w0 + jnp.tanh(wl) @ w_w2)) - 0.5
    decay = jnp.exp(-jnp.exp(w_log))
    a = jax.nn.sigmoid(a0 + al @ w_a2)
    g = jax.nn.sigmoid(gl) @ w_g2

    def heads(t):
        return t.reshape(B, T, N_HEADS_RWKV, HEAD_DIM)
    kk = heads(k * k_k)
    kk = kk / jnp.maximum(jnp.sqrt(jnp.sum(kk * kk, axis=-1, keepdims=True)), 1e-12)
    k = k * (1.0 + (a - 1.0) * k_a)
    r_h, k_h, v_h = heads(r), heads(k), heads(v)
    y, s_fin = _wkv_scan(r_h, heads(decay), k_h, v_h, kk, heads(a), s0.astype(jnp.float32))
    mean = jnp.mean(y, axis=-1, keepdims=True)
    var = jnp.mean(jnp.square(y - mean), axis=-1, keepdims=True)
    y = ((y - mean) * lax.rsqrt(var + GN_EPS)).reshape(B, T, C_RWKV) * lnx_g + lnx_b
    bonus = jnp.sum(r_h * k_h * r_k, axis=-1, keepdims=True) * v_h
    y = (y + bonus.reshape(B, T, C_RWKV)) * g
    return y, s_fin, pb[:, -1]


def _peer(h, w_pq, sub_keys, expert_u, expert_v):
    n_tok = h.shape[0]
    chunk = min(PEER_CHUNK, n_tok)
    pad = (-n_tok) % chunk
    hc = jnp.pad(h, ((0, pad), (0, 0))).reshape(-1, chunk, D_MODEL)

    def one(hb):
        q = (hb @ w_pq).reshape(chunk, PEER_HEADS, 2, PEER_HALF)
        s = jnp.einsum('nhpc,hpkc->nhpk', q, sub_keys).astype(jnp.float32)
        s1, i1 = lax.top_k(s[:, :, 0], PEER_TOPK)
        s2, i2 = lax.top_k(s[:, :, 1], PEER_TOPK)
        cand = (s1[..., :, None] + s2[..., None, :]).reshape(chunk, PEER_HEADS, PEER_TOPK * PEER_TOPK)
        cid = (i1[..., :, None] * PEER_KEYS + i2[..., None, :]).reshape(chunk, PEER_HEADS, PEER_TOPK * PEER_TOPK)
        top_s, pos = lax.top_k(cand, PEER_TOPK)
        eid = jnp.take_along_axis(cid, pos, axis=-1)
        gate = jax.nn.softmax(top_s, axis=-1)
        act = jax.nn.gelu(jnp.einsum('nd,nhkd->nhk', hb, expert_u[eid]).astype(jnp.float32), approximate=False)
        return jnp.einsum('nhk,nhkd->nd', gate * act, expert_v[eid].astype(jnp.float32))

    return lax.map(one, hc).reshape(-1, D_MODEL)[:n_tok]


def _layer(x, c, k_buf, v_buf, wkv0, shift0, rel_bias, p):
    B, T, _ = x.shape
    mod = jax.nn.silu(c.astype(jnp.float32)) @ p['ada_w'] + p['ada_b']
    sh1, sc1, g1, sh2, sc2, g2 = jnp.split(mod[:, None, :], 6, axis=-1)
    h = _rms(x, p['norm1_g']) * (1.0 + sc1) + sh1
    proj = h @ p['w_in']
    q = _rms(proj[..., :C_ATTN].reshape(B, T, N_HEADS_ATTN, HEAD_DIM), p['q_norm_g'])
    k = _rms(proj[..., C_ATTN:2 * C_ATTN].reshape(B, T, N_HEADS_ATTN, HEAD_DIM), p['k_norm_g'])
    v = proj[..., 2 * C_ATTN:3 * C_ATTN].reshape(B, T, N_HEADS_ATTN, HEAD_DIM).astype(jnp.float32)
    if k_buf is None:
        o_attn = _dilated_attn_prompt(q, k, v, rel_bias)
        keep = min(MAX_WINDOW, T)
        k_rows, v_rows = k[:, T - keep:], v[:, T - keep:]
        wkv0 = jnp.zeros((B, N_HEADS_RWKV, HEAD_DIM, HEAD_DIM), jnp.float32)
        shift0 = jnp.zeros((B, COLS_RWKV), jnp.float32)
    else:
        o_attn = _dilated_attn_sample(q, k, v, k_buf, v_buf, rel_bias)
        k_rows, v_rows = k, v
    y_rwkv, s_fin, shift_new = _rwkv_mixer(
        proj[..., 3 * C_ATTN:], shift0, wkv0, p['mu_shift'], p['w0'], p['w_w2'], p['a0'], p['w_a2'],
        p['w_g2'], p['k_k'], p['k_a'], p['r_k'], p['lnx_g'], p['lnx_b'])
    mix = jnp.concatenate([o_attn.reshape(B, T, C_ATTN), y_rwkv], axis=-1) @ p['w_out']
    x = x + g1 * mix
    h2 = _rms(x, p['norm2_g']) * (1.0 + sc2) + sh2
    ffn = _peer(h2.reshape(B * T, D_MODEL), p['w_peer_q'], p['peer_sub_keys'], p['expert_u'], p['expert_v'])
    x = x + g2 * ffn.reshape(B, T, D_MODEL)
    return x, k_rows, v_rows, s_fin, shift_new


def setup_inputs(seed: int = 0) -> dict:
    key = jax.random.key(seed)
    ks = jax.random.split(key, 32)
    f32 = jnp.float32

    def nrm(i, shape, scale):
        return jax.random.normal(ks[i], shape, f32) * scale

    def unif(i, shape, lo, hi):
        return jax.random.uniform(ks[i], shape, f32, lo, hi)

    win_buf = min(MAX_WINDOW, PAST_LEN)
    d = D_MODEL
    return {
        'x_prompt': nrm(0, (BATCH, SEQ, d), 1.0),
        'x_sample': nrm(1, (DEC_BATCH, DEC_SEQ, d), 1.0),
        'c_prompt': nrm(2, (BATCH, d), 1.0),
        'c_sample': nrm(3, (DEC_BATCH, d), 1.0),
        'cache_k_win': nrm(4, (DEPTH, DEC_BATCH, win_buf, N_HEADS_ATTN, HEAD_DIM), 1.0),
        'cache_v_win': nrm(5, (DEPTH, DEC_BATCH, win_buf, N_HEADS_ATTN, HEAD_DIM), 1.0),
        'state_wkv': nrm(6, (DEPTH, DEC_BATCH, N_HEADS_RWKV, HEAD_DIM, HEAD_DIM), 0.5),
        'state_shift': nrm(7, (DEPTH, DEC_BATCH, COLS_RWKV), 1.0),
        'ada_w': nrm(8, (DEPTH, d, 6 * d), 0.5 * d ** -0.5),
        'ada_b': nrm(9, (DEPTH, 6 * d), 0.02),
        'norm1_g': 1.0 + nrm(10, (DEPTH, d), 0.02),
        'norm2_g': 1.0 + nrm(11, (DEPTH, d), 0.02),
        'w_in': nrm(12, (DEPTH, d, D_IN), d ** -0.5),
        'q_norm_g': 1.0 + nrm(13, (DEPTH, HEAD_DIM), 0.02),
        'k_norm_g': 1.0 + nrm(14, (DEPTH, HEAD_DIM), 0.02),
        'rel_bias': nrm(15, (N_BUCKETS, N_HEADS_ATTN), 0.5),
        'mu_shift': unif(16, (DEPTH, COLS_RWKV), 0.0, 1.0),
        'w0': unif(17, (DEPTH, C_RWKV), -5.0, 1.0),
        'w_w2': nrm(18, (DEPTH, LORA_DECAY, C_RWKV), 0.1),
        'a0': nrm(19, (DEPTH, C_RWKV), 0.5),
        'w_a2': nrm(20, (DEPTH, LORA_ICLR, C_RWKV), 0.1),
        'w_g2': nrm(21, (DEPTH, LORA_GATE, C_RWKV), LORA_GATE ** -0.5),
        'k_k': 0.85 + nrm(22, (DEPTH, C_RWKV), 0.05),
        'k_a': 1.0 + nrm(23, (DEPTH, C_RWKV), 0.05),
        'r_k': nrm(24, (DEPTH, N_HEADS_RWKV, HEAD_DIM), 0.1),
        'lnx_g': 1.0 + nrm(25, (DEPTH, C_RWKV), 0.02),
        'lnx_b': nrm(26, (DEPTH, C_RWKV), 0.02),
        'w_out': nrm(27, (DEPTH, d, d), d ** -0.5),
        'w_peer_q': nrm(28, (DEPTH, d, PEER_HEADS * PEER_QDIM), d ** -0.5),
        'peer_sub_keys': nrm(29, (DEPTH, PEER_HEADS, 2, PEER_KEYS, PEER_HALF), PEER_HALF ** -0.5),
        'expert_u': nrm(30, (DEPTH, PEER_EXPERTS, d), d ** -0.5),
        'expert_v': nrm(31, (DEPTH, PEER_EXPERTS, d), PEER_HEADS ** -0.5),
    }


def reference(x_prompt, x_sample, c_prompt, c_sample, cache_k_win, cache_v_win, state_wkv, state_shift,
              ada_w, ada_b, norm1_g, norm2_g, w_in, q_norm_g, k_norm_g, rel_bias, mu_shift, w0, w_w2, a0,
              w_a2, w_g2, k_k, k_a, r_k, lnx_g, lnx_b, w_out, w_peer_q, peer_sub_keys, expert_u, expert_v):
    xp, xs = x_prompt, x_sample
    kp_l, vp_l, sp_l, hp_l = [], [], [], []
    ks_l, vs_l, ss_l, hs_l = [], [], [], []
    for l in range(DEPTH):
        p = {
            'ada_w': ada_w[l], 'ada_b': ada_b[l], 'norm1_g': norm1_g[l], 'norm2_g': norm2_g[l],
            'w_in': w_in[l], 'q_norm_g': q_norm_g[l], 'k_norm_g': k_norm_g[l], 'mu_shift': mu_shift[l],
            'w0': w0[l], 'w_w2': w_w2[l], 'a0': a0[l], 'w_a2': w_a2[l], 'w_g2': w_g2[l], 'k_k': k_k[l],
            'k_a': k_a[l], 'r_k': r_k[l], 'lnx_g': lnx_g[l], 'lnx_b': lnx_b[l], 'w_out': w_out[l],
            'w_peer_q': w_peer_q[l], 'peer_sub_keys': peer_sub_keys[l], 'expert_u': expert_u[l],
            'expert_v': expert_v[l],
        }
        xp, kp, vp, sp, hp = _layer(xp, c_prompt, None, None, None, None, rel_bias, p)
        xs, kn, vn, sn, hn = _layer(xs, c_sample, cache_k_win[l], cache_v_win[l], state_wkv[l],
                                    state_shift[l], rel_bias, p)
        kp_l.append(kp); vp_l.append(vp); sp_l.append(sp); hp_l.append(hp)
        ks_l.append(kn); vs_l.append(vn); ss_l.append(sn); hs_l.append(hn)
    return (xp, xs, jnp.stack(kp_l), jnp.stack(vp_l), jnp.stack(sp_l), jnp.stack(hp_l),
            jnp.stack(ks_l), jnp.stack(vs_l), jnp.stack(ss_l), jnp.stack(hs_l))
```

```python
import math
import jax, jax.numpy as jnp
from jax import lax
import numpy as np
from jax.experimental import pallas as pl

D_MODEL = 1024
DEPTH = 1
HEAD_DIM = 64
N_HEADS_ATTN = 8
N_HEADS_RWKV = 8
C_ATTN = N_HEADS_ATTN * HEAD_DIM
C_RWKV = N_HEADS_RWKV * HEAD_DIM
DILATIONS = ((128, 1), (512, 4), (2048, 16))
MAX_WINDOW = 2048
N_BUCKETS = 32
MAX_DISTANCE = 2048
LORA_DECAY = 32
LORA_ICLR = 32
LORA_GATE = 64
COLS_RWKV = 3 * C_RWKV + LORA_DECAY + LORA_ICLR + LORA_GATE
D_IN = 3 * C_ATTN + COLS_RWKV
PEER_HEADS = 8
PEER_KEYS = 128
PEER_QDIM = 256
PEER_HALF = PEER_QDIM // 2
PEER_TOPK = 16
PEER_CHUNK = 256
NORM_EPS = 1e-6
GN_EPS = 64e-5
NEG_INF = -1e30
ATTN_SCALE = HEAD_DIM ** -0.5


def _rms(x, g):
    x32 = x.astype(jnp.float32)
    return x32 * lax.rsqrt(jnp.mean(x32 * x32, axis=-1, keepdims=True) + NORM_EPS) * g


def _t5_bucket(dist):
    dist = np.asarray(dist, dtype=np.int64)
    max_exact = N_BUCKETS // 2
    safe = np.maximum(dist, 1) / max_exact
    large = max_exact + (np.log(safe) / math.log(MAX_DISTANCE / max_exact) * (N_BUCKETS - max_exact)).astype(np.int64)
    large = np.minimum(large, N_BUCKETS - 1)
    return np.where(dist < max_exact, dist, large).astype(np.int32)


def _branch_prompt(q, k, v, rel_bias, window, dil):
    B, T, H, C = q.shape
    n = window // dil
    L = T + (-T) % window
    G = L // window

    def blocks(a):
        a = jnp.pad(a, ((0, 0), (0, L - T), (0, 0), (0, 0)))
        a = a.reshape(B, L // dil, dil, H, C).transpose(0, 2, 1, 3, 4)
        return a.reshape(B, dil, G, n, H, C)

    def with_prev(a):
        prev = jnp.pad(a[:, :, :-1], ((0, 0), (0, 0), (1, 0), (0, 0), (0, 0), (0, 0)))
        return jnp.concatenate([prev, a], axis=3)

    qb = blocks(q)
    kb = with_prev(blocks(k))
    vb = with_prev(blocks(v))
    qi = np.arange(n)[:, None]
    ki = np.arange(2 * n)[None, :]
    j = n + qi - ki
    band = (j >= 0) & (j <= n)
    pad_key = (np.arange(G) == 0)[:, None, None] & (ki < n)[None]
    mask = band[None] & ~pad_key
    bias = rel_bias[_t5_bucket(np.clip(j, 0, n) * dil)].transpose(2, 0, 1).astype(jnp.float32)
    logits = jnp.einsum('brgqhc,brgshc->brghqs', qb, kb).astype(jnp.float32) * ATTN_SCALE + bias
    logits = jnp.where(mask[None, None, :, None], logits, NEG_INF)
    m = jnp.max(logits, axis=-1, keepdims=True)
    e = jnp.exp(logits - m)
    s = jnp.sum(e, axis=-1)
    o = jnp.einsum('brghqs,brgshc->brgqhc', e, vb.astype(jnp.float32)) / s.transpose(0, 1, 2, 4, 3)[..., None]
    lse = (m[..., 0] + jnp.log(s)).transpose(0, 1, 2, 4, 3)
    o = o.reshape(B, dil, L // dil, H, C).transpose(0, 2, 1, 3, 4).reshape(B, L, H, C)[:, :T]
    lse = lse.reshape(B, dil, L // dil, H).transpose(0, 2, 1, 3).reshape(B, L, H)[:, :T]
    return o, lse


def _branch_sample(q, k_all, v_all, rel_bias, window, dil, lb):
    S = q.shape[1]
    n = window // dil
    jj = np.arange(n + 1)
    idx = lb + np.arange(S)[:, None] - jj[None, :] * dil
    valid = idx >= 0
    idx_c = np.maximum(idx, 0)
    ks = k_all[:, idx_c].astype(jnp.float32)
    vs = v_all[:, idx_c].astype(jnp.float32)
    bias = rel_bias[_t5_bucket(jj * dil)].T.astype(jnp.float32)
    logits = jnp.einsum('bqhc,bqjhc->bhqj', q, ks).astype(jnp.float32) * ATTN_SCALE + bias[None, :, None, :]
    logits = jnp.where(valid[None, None], logits, NEG_INF)
    m = jnp.max(logits, axis=-1, keepdims=True)
    e = jnp.exp(logits - m)
    s = jnp.sum(e, axis=-1)
    o = jnp.einsum('bhqj,bqjhc->bqhc', e, vs) / s.transpose(0, 2, 1)[..., None]
    lse = (m[..., 0] + jnp.log(s)).transpose(0, 2, 1)
    return o, lse


def _merge_branches(outs, lses):
    wts = jax.nn.softmax(jnp.stack(lses), axis=0)
    return jnp.einsum('ibth,ibthc->bthc', wts, jnp.stack(outs))


def _dilated_attn_prompt(q, k, v, rel_bias):
    outs, lses = [], []
    for window, dil in DILATIONS:
        o, lse = _branch_prompt(q, k, v, rel_bias, window, dil)
        outs.append(o)
        lses.append(lse)
    return _merge_branches(outs, lses)


def _dilated_attn_sample(q, k_new, v_new, k_buf, v_buf, rel_bias):
    lb = k_buf.shape[1]
    k_all = jnp.concatenate([k_buf.astype(jnp.float32), k_new.astype(jnp.float32)], axis=1)
    v_all = jnp.concatenate([v_buf.astype(jnp.float32), v_new.astype(jnp.float32)], axis=1)
    outs, lses = [], []
    for window, dil in DILATIONS:
        o, lse = _branch_sample(q, k_all, v_all, rel_bias, window, dil, lb)
        outs.append(o)
        lses.append(lse)
    return _merge_branches(outs, lses)


def _wkv_scan(r, w, k, v, kk, a, s0):
    def step(S, inp):
        r_t, w_t, k_t, v_t, kk_t, a_t = inp
        sa = jnp.einsum('bhvk,bhk->bhv', S, -kk_t)
        S = S * w_t[:, :, None, :] + sa[..., None] * (kk_t * a_t)[:, :, None, :] + v_t[..., None] * k_t[:, :, None, :]
        return S, jnp.einsum('bhvk,bhk->bhv', S, r_t)
    xs = tuple(jnp.moveaxis(t, 1, 0) for t in (r, w, k, v, kk, a))
    s_fin, y = lax.scan(step, s0, xs)
    return jnp.moveaxis(y, 0, 1), s_fin


def _rwkv_mixer(pb, shift0, s0, mu, w0, w_w2, a0, w_a2, w_g2, k_k, k_a, r_k, lnx_g, lnx_b):
    B, T, _ = pb.shape
    pb = pb.astype(jnp.float32)
    prev = jnp.concatenate([shift0[:, None, :].astype(jnp.float32), pb[:, :-1]], axis=1)
    xm = pb + (prev - pb) * mu
    c = C_RWKV
    r, k, v = xm[..., :c], xm[..., c:2 * c], xm[..., 2 * c:3 * c]
    o = 3 * c
    wl = xm[..., o:o + LORA_DECAY]
    al = xm[..., o + LORA_DECAY:o + LORA_DECAY + LORA_ICLR]
    gl = xm[..., o + LORA_DECAY + LORA_ICLR:]
    w_log = -jax.nn.softplus(-(w0 + jnp.tanh(wl) @ w_w2)) - 0.5
    decay = jnp.exp(-jnp.exp(w_log))
    a = jax.nn.sigmoid(a0 + al @ w_a2)
    g = jax.nn.sigmoid(gl) @ w_g2

    def heads(t):
        return t.reshape(B, T, N_HEADS_RWKV, HEAD_DIM)
    kk = heads(k * k_k)
    kk = kk / jnp.maximum(jnp.sqrt(jnp.sum(kk * kk, axis=-1, keepdims=True)), 1e-12)
    k = k * (1.0 + (a - 1.0) * k_a)
    r_h, k_h, v_h = heads(r), heads(k), heads(v)
    y, s_fin = _wkv_scan(r_h, heads(decay), k_h, v_h, kk, heads(a), s0.astype(jnp.float32))
    mean = jnp.mean(y, axis=-1, keepdims=True)
    var = jnp.mean(jnp.square(y - mean), axis=-1, keepdims=True)
    y = ((y - mean) * lax.rsqrt(var + GN_EPS)).reshape(B, T, C_RWKV) * lnx_g + lnx_b
    bonus = jnp.sum(r_h * k_h * r_k, axis=-1, keepdims=True) * v_h
    y = (y + bonus.reshape(B, T, C_RWKV)) * g
    return y, s_fin, pb[:, -1]


def _peer(h, w_pq, sub_keys, expert_u, expert_v):
    n_tok = h.shape[0]
    chunk = min(PEER_CHUNK, n_tok)
    pad = (-n_tok) % chunk
    hc = jnp.pad(h, ((0, pad), (0, 0))).reshape(-1, chunk, D_MODEL)

    def one(hb):
        q = (hb @ w_pq).reshape(chunk, PEER_HEADS, 2, PEER_HALF)
        s = jnp.einsum('nhpc,hpkc->nhpk', q, sub_keys).astype(jnp.float32)
        s1, i1 = lax.top_k(s[:, :, 0], PEER_TOPK)
        s2, i2 = lax.top_k(s[:, :, 1], PEER_TOPK)
        cand = (s1[..., :, None] + s2[..., None, :]).reshape(chunk, PEER_HEADS, PEER_TOPK * PEER_TOPK)
        cid = (i1[..., :, None] * PEER_KEYS + i2[..., None, :]).reshape(chunk, PEER_HEADS, PEER_TOPK * PEER_TOPK)
        top_s, pos = lax.top_k(cand, PEER_TOPK)
        eid = jnp.take_along_axis(cid, pos, axis=-1)
        gate = jax.nn.softmax(top_s, axis=-1)
        act = jax.nn.gelu(jnp.einsum('nd,nhkd->nhk', hb, expert_u[eid]).astype(jnp.float32), approximate=False)
        return jnp.einsum('nhk,nhkd->nd', gate * act, expert_v[eid].astype(jnp.float32))

    return lax.map(one, hc).reshape(-1, D_MODEL)[:n_tok]


def _resid_kernel(x_ref, g_ref, f_ref, o_ref):
    o_ref[...] = x_ref[...] + g_ref[...] * f_ref[...]


def _resid(x, g, f):
    B, T, D = x.shape
    tb = min(T, 512)
    return pl.pallas_call(
        _resid_kernel,
        grid=(B, T // tb),
        in_specs=[pl.BlockSpec((1, tb, D), lambda b, t: (b, t, 0)),
                  pl.BlockSpec((1, 1, D), lambda b, t: (b, 0, 0)),
                  pl.BlockSpec((1, tb, D), lambda b, t: (b, t, 0))],
        out_specs=pl.BlockSpec((1, tb, D), lambda b, t: (b, t, 0)),
        out_shape=jax.ShapeDtypeStruct(x.shape, x.dtype),
    )(x, g, f)


def _layer(x, c, k_buf, v_buf, wkv0, shift0, rel_bias, p):
    B, T, _ = x.shape
    mod = jax.nn.silu(c.astype(jnp.float32)) @ p['ada_w'] + p['ada_b']
    sh1, sc1, g1, sh2, sc2, g2 = jnp.split(mod[:, None, :], 6, axis=-1)
    h = _rms(x, p['norm1_g']) * (1.0 + sc1) + sh1
    proj = h @ p['w_in']
    q = _rms(proj[..., :C_ATTN].reshape(B, T, N_HEADS_ATTN, HEAD_DIM), p['q_norm_g'])
    k = _rms(proj[..., C_ATTN:2 * C_ATTN].reshape(B, T, N_HEADS_ATTN, HEAD_DIM), p['k_norm_g'])
    v = proj[..., 2 * C_ATTN:3 * C_ATTN].reshape(B, T, N_HEADS_ATTN, HEAD_DIM).astype(jnp.float32)
    if k_buf is None:
        o_attn = _dilated_attn_prompt(q, k, v, rel_bias)
        keep = min(MAX_WINDOW, T)
        k_rows, v_rows = k[:, T - keep:], v[:, T - keep:]
        wkv0 = jnp.zeros((B, N_HEADS_RWKV, HEAD_DIM, HEAD_DIM), jnp.float32)
        shift0 = jnp.zeros((B, COLS_RWKV), jnp.float32)
    else:
        o_attn = _dilated_attn_sample(q, k, v, k_buf, v_buf, rel_bias)
        k_rows, v_rows = k, v
    y_rwkv, s_fin, shift_new = _rwkv_mixer(
        proj[..., 3 * C_ATTN:], shift0, wkv0, p['mu_shift'], p['w0'], p['w_w2'], p['a0'], p['w_a2'],
        p['w_g2'], p['k_k'], p['k_a'], p['r_k'], p['lnx_g'], p['lnx_b'])
    mix = jnp.concatenate([o_attn.reshape(B, T, C_ATTN), y_rwkv], axis=-1) @ p['w_out']
    x = x + g1 * mix
    h2 = _rms(x, p['norm2_g']) * (1.0 + sc2) + sh2
    ffn = _peer(h2.reshape(B * T, D_MODEL), p['w_peer_q'], p['peer_sub_keys'], p['expert_u'], p['expert_v'])
    x = _resid(x, g2, ffn.reshape(B, T, D_MODEL))
    return x, k_rows, v_rows, s_fin, shift_new


def kernel(x_prompt, x_sample, c_prompt, c_sample, cache_k_win, cache_v_win, state_wkv, state_shift,
           ada_w, ada_b, norm1_g, norm2_g, w_in, q_norm_g, k_norm_g, rel_bias, mu_shift, w0, w_w2, a0,
           w_a2, w_g2, k_k, k_a, r_k, lnx_g, lnx_b, w_out, w_peer_q, peer_sub_keys, expert_u, expert_v):
    xp, xs = x_prompt, x_sample
    kp_l, vp_l, sp_l, hp_l = [], [], [], []
    ks_l, vs_l, ss_l, hs_l = [], [], [], []
    names = ('ada_w', 'ada_b', 'norm1_g', 'norm2_g', 'w_in', 'q_norm_g', 'k_norm_g', 'mu_shift', 'w0', 'w_w2',
             'a0', 'w_a2', 'w_g2', 'k_k', 'k_a', 'r_k', 'lnx_g', 'lnx_b', 'w_out', 'w_peer_q', 'peer_sub_keys',
             'expert_u', 'expert_v')
    vals = (ada_w, ada_b, norm1_g, norm2_g, w_in, q_norm_g, k_norm_g, mu_shift, w0, w_w2, a0, w_a2, w_g2, k_k,
            k_a, r_k, lnx_g, lnx_b, w_out, w_peer_q, peer_sub_keys, expert_u, expert_v)
    for l in range(DEPTH):
        p = {n: v[l] for n, v in zip(names, vals)}
        xp, kp, vp, sp, hp = _layer(xp, c_prompt, None, None, None, None, rel_bias, p)
        xs, kn, vn, sn, hn = _layer(xs, c_sample, cache_k_win[l], cache_v_win[l], state_wkv[l],
                                    state_shift[l], rel_bias, p)
        kp_l.append(kp); vp_l.append(vp); sp_l.append(sp); hp_l.append(hp)
        ks_l.append(kn); vs_l.append(vn); ss_l.append(sn); hs_l.append(hn)
    return (xp, xs, jnp.stack(kp_l), jnp.stack(vp_l), jnp.stack(sp_l), jnp.stack(hp_l),
            jnp.stack(ks_l), jnp.stack(vs_l), jnp.stack(ss_l), jnp.stack(hs_l))
```

```python
import functools
import math
import jax, jax.numpy as jnp
from jax import lax
import numpy as np
from jax.experimental import pallas as pl
from jax.experimental.pallas import tpu as pltpu

D_MODEL = 1024
DEPTH = 1
HEAD_DIM = 64
N_HEADS_ATTN = 8
N_HEADS_RWKV = 8
C_ATTN = N_HEADS_ATTN * HEAD_DIM
C_RWKV = N_HEADS_RWKV * HEAD_DIM
DILATIONS = ((128, 1), (512, 4), (2048, 16))
MAX_WINDOW = 2048
N_BUCKETS = 32
MAX_DISTANCE = 2048
LORA_DECAY = 32
LORA_ICLR = 32
LORA_GATE = 64
COLS_RWKV = 3 * C_RWKV + LORA_DECAY + LORA_ICLR + LORA_GATE
D_IN = 3 * C_ATTN + COLS_RWKV
PEER_HEADS = 8
PEER_KEYS = 128
PEER_QDIM = 256
PEER_HALF = PEER_QDIM // 2
PEER_TOPK = 16
PEER_CHUNK = 256
NORM_EPS = 1e-6
GN_EPS = 64e-5
NEG_INF = -1e30
ATTN_SCALE = HEAD_DIM ** -0.5


def _rms(x, g):
    x32 = x.astype(jnp.float32)
    return x32 * lax.rsqrt(jnp.mean(x32 * x32, axis=-1, keepdims=True) + NORM_EPS) * g


def _t5_bucket(dist):
    dist = np.asarray(dist, dtype=np.int64)
    max_exact = N_BUCKETS // 2
    safe = np.maximum(dist, 1) / max_exact
    large = max_exact + (np.log(safe) / math.log(MAX_DISTANCE / max_exact) * (N_BUCKETS - max_exact)).astype(np.int64)
    large = np.minimum(large, N_BUCKETS - 1)
    return np.where(dist < max_exact, dist, large).astype(np.int32)


def _branch_prompt(q, k, v, rel_bias, window, dil):
    B, T, H, C = q.shape
    n = window // dil
    L = T + (-T) % window
    G = L // window

    def blocks(a):
        a = jnp.pad(a, ((0, 0), (0, L - T), (0, 0), (0, 0)))
        a = a.reshape(B, L // dil, dil, H, C).transpose(0, 2, 1, 3, 4)
        return a.reshape(B, dil, G, n, H, C)

    def with_prev(a):
        prev = jnp.pad(a[:, :, :-1], ((0, 0), (0, 0), (1, 0), (0, 0), (0, 0), (0, 0)))
        return jnp.concatenate([prev, a], axis=3)

    qb = blocks(q)
    kb = with_prev(blocks(k))
    vb = with_prev(blocks(v))
    qi = np.arange(n)[:, None]
    ki = np.arange(2 * n)[None, :]
    j = n + qi - ki
    band = (j >= 0) & (j <= n)
    pad_key = (np.arange(G) == 0)[:, None, None] & (ki < n)[None]
    mask = band[None] & ~pad_key
    bias = rel_bias[_t5_bucket(np.clip(j, 0, n) * dil)].transpose(2, 0, 1).astype(jnp.float32)
    logits = jnp.einsum('brgqhc,brgshc->brghqs', qb, kb).astype(jnp.float32) * ATTN_SCALE + bias
    logits = jnp.where(mask[None, None, :, None], logits, NEG_INF)
    m = jnp.max(logits, axis=-1, keepdims=True)
    e = jnp.exp(logits - m)
    s = jnp.sum(e, axis=-1)
    o = jnp.einsum('brghqs,brgshc->brgqhc', e, vb.astype(jnp.float32)) / s.transpose(0, 1, 2, 4, 3)[..., None]
    lse = (m[..., 0] + jnp.log(s)).transpose(0, 1, 2, 4, 3)
    o = o.reshape(B, dil, L // dil, H, C).transpose(0, 2, 1, 3, 4).reshape(B, L, H, C)[:, :T]
    lse = lse.reshape(B, dil, L // dil, H).transpose(0, 2, 1, 3).reshape(B, L, H)[:, :T]
    return o, lse


def _branch_sample(q, k_all, v_all, rel_bias, window, dil, lb):
    S = q.shape[1]
    n = window // dil
    jj = np.arange(n + 1)
    idx = lb + np.arange(S)[:, None] - jj[None, :] * dil
    valid = idx >= 0
    idx_c = np.maximum(idx, 0)
    ks = k_all[:, idx_c].astype(jnp.float32)
    vs = v_all[:, idx_c].astype(jnp.float32)
    bias = rel_bias[_t5_bucket(jj * dil)].T.astype(jnp.float32)
    logits = jnp.einsum('bqhc,bqjhc->bhqj', q, ks).astype(jnp.float32) * ATTN_SCALE + bias[None, :, None, :]
    logits = jnp.where(valid[None, None], logits, NEG_INF)
    m = jnp.max(logits, axis=-1, keepdims=True)
    e = jnp.exp(logits - m)
    s = jnp.sum(e, axis=-1)
    o = jnp.einsum('bhqj,bqjhc->bqhc', e, vs) / s.transpose(0, 2, 1)[..., None]
    lse = (m[..., 0] + jnp.log(s)).transpose(0, 2, 1)
    return o, lse


def _merge_branches(outs, lses):
    wts = jax.nn.softmax(jnp.stack(lses), axis=0)
    return jnp.einsum('ibth,ibthc->bthc', wts, jnp.stack(outs))


def _dilated_attn_prompt(q, k, v, rel_bias):
    outs, lses = [], []
    for window, dil in DILATIONS:
        o, lse = _branch_prompt(q, k, v, rel_bias, window, dil)
        outs.append(o)
        lses.append(lse)
    return _merge_branches(outs, lses)


def _dilated_attn_sample(q, k_new, v_new, k_buf, v_buf, rel_bias):
    lb = k_buf.shape[1]
    k_all = jnp.concatenate([k_buf.astype(jnp.float32), k_new.astype(jnp.float32)], axis=1)
    v_all = jnp.concatenate([v_buf.astype(jnp.float32), v_new.astype(jnp.float32)], axis=1)
    outs, lses = [], []
    for window, dil in DILATIONS:
        o, lse = _branch_sample(q, k_all, v_all, rel_bias, window, dil, lb)
        outs.append(o)
        lses.append(lse)
    return _merge_branches(outs, lses)


def _wkv_scan(r, w, k, v, kk, a, s0):
    def step(S, inp):
        r_t, w_t, k_t, v_t, kk_t, a_t = inp
        sa = jnp.einsum('bhvk,bhk->bhv', S, -kk_t)
        S = S * w_t[:, :, None, :] + sa[..., None] * (kk_t * a_t)[:, :, None, :] + v_t[..., None] * k_t[:, :, None, :]
        return S, jnp.einsum('bhvk,bhk->bhv', S, r_t)
    xs = tuple(jnp.moveaxis(t, 1, 0) for t in (r, w, k, v, kk, a))
    s_fin, y = lax.scan(step, s0, xs)
    return jnp.moveaxis(y, 0, 1), s_fin


def _rwkv_mixer(pb, shift0, s0, mu, w0, w_w2, a0, w_a2, w_g2, k_k, k_a, r_k, lnx_g, lnx_b):
    B, T, _ = pb.shape
    pb = pb.astype(jnp.float32)
    prev = jnp.concatenate([shift0[:, None, :].astype(jnp.float32), pb[:, :-1]], axis=1)
    xm = pb + (prev - pb) * mu
    c = C_RWKV
    r, k, v = xm[..., :c], xm[..., c:2 * c], xm[..., 2 * c:3 * c]
    o = 3 * c
    wl = xm[..., o:o + LORA_DECAY]
    al = xm[..., o + LORA_DECAY:o + LORA_DECAY + LORA_ICLR]
    gl = xm[..., o + LORA_DECAY + LORA_ICLR:]
    w_log = -jax.nn.softplus(-(w0 + jnp.tanh(wl) @ w_w2)) - 0.5
    decay = jnp.exp(-jnp.exp(w_log))
    a = jax.nn.sigmoid(a0 + al @ w_a2)
    g = jax.nn.sigmoid(gl) @ w_g2

    def heads(t):
        return t.reshape(B, T, N_HEADS_RWKV, HEAD_DIM)
    kk = heads(k * k_k)
    kk = kk / jnp.maximum(jnp.sqrt(jnp.sum(kk * kk, axis=-1, keepdims=True)), 1e-12)
    k = k * (1.0 + (a - 1.0) * k_a)
    r_h, k_h, v_h = heads(r), heads(k), heads(v)
    y, s_fin = _wkv_scan(r_h, heads(decay), k_h, v_h, kk, heads(a), s0.astype(jnp.float32))
    mean = jnp.mean(y, axis=-1, keepdims=True)
    var = jnp.mean(jnp.square(y - mean), axis=-1, keepdims=True)
    y = ((y - mean) * lax.rsqrt(var + GN_EPS)).reshape(B, T, C_RWKV) * lnx_g + lnx_b
    bonus = jnp.sum(r_h * k_h * r_k, axis=-1, keepdims=True) * v_h
    y = (y + bonus.reshape(B, T, C_RWKV)) * g
    return y, s_fin, pb[:, -1]


LANES = 128
SUBLANES = 8


def _gelu_exact(x):
    return 0.5 * x * (1.0 + lax.erf(x * (2.0 ** -0.5)))


def _peer_expert_kernel(eid0_ref, eidn_ref, h_ref, gate_ref, tab_ref, out_ref, buf, sem, *, tok, rows, chunks):
    i = pl.program_id(0)
    n = pl.num_programs(0)
    erow = 2 * chunks
    slot_rows = tok * rows * erow
    groups = rows // SUBLANES
    slot = i % 2

    def row_copy(eid_ref, t, r, dst_slot):
        e = eid_ref[t, r]
        src = tab_ref.at[pl.ds(pl.multiple_of(e * erow, erow), erow)]
        dst = buf.at[pl.ds(pl.multiple_of(dst_slot * slot_rows + (t * rows + r) * erow, erow), erow)]
        return pltpu.make_async_copy(src, dst, sem.at[dst_slot])

    def wait_slot(s):
        pltpu.make_async_copy(tab_ref.at[pl.ds(0, slot_rows)],
                              buf.at[pl.ds(pl.multiple_of(s * slot_rows, slot_rows), slot_rows)],
                              sem.at[s]).wait()

    @pl.when(i == 0)
    def _():
        def prime(t, c):
            for r in range(rows):
                row_copy(eid0_ref, t, r, 0).start()
            return c
        lax.fori_loop(0, tok, prime, 0)

    wait_slot(slot)
    lane = lax.broadcasted_iota(jnp.int32, (SUBLANES, LANES), 1)

    def token(t, c):
        for r in range(rows):
            row_copy(eidn_ref, t, r, 1 - slot).start()
        base = slot * slot_rows + t * (rows * erow)
        hb = [jnp.broadcast_to(h_ref[t, pl.ds(k, 1), :], (SUBLANES, LANES)) for k in range(chunks)]
        s_tile = jnp.zeros((SUBLANES, LANES), jnp.float32)
        for g in range(groups):
            acc = None
            for k in range(chunks):
                u = buf[pl.ds(base + g * SUBLANES * erow + k, SUBLANES, stride=erow), :]
                p = u * hb[k]
                acc = p if acc is None else acc + p
            col = jnp.sum(acc, axis=1, keepdims=True)
            s_tile = jnp.where(lane == g, col, s_tile)
        w_tile = _gelu_exact(s_tile) * gate_ref[t]
        accs = [None] * chunks
        for g in range(groups):
            wg = jnp.broadcast_to(w_tile[:, g:g + 1], (SUBLANES, LANES))
            for k in range(chunks):
                v = buf[pl.ds(base + g * SUBLANES * erow + chunks + k, SUBLANES, stride=erow), :]
                p = wg * v
                accs[k] = p if accs[k] is None else accs[k] + p
        out_ref[t] = jnp.concatenate([jnp.sum(a, axis=0, keepdims=True) for a in accs], axis=0)
        return c

    lax.fori_loop(0, tok, token, 0)

    @pl.when(i == n - 1)
    def _():
        wait_slot(1 - slot)


def peer_experts(h, eid, gate, expert_u, expert_v, *, tok=8):
    N, D = h.shape
    R = eid.shape[1]
    E = expert_u.shape[0]
    chunks = D // LANES
    assert N % tok == 0 and R % SUBLANES == 0 and D % LANES == 0
    groups = R // SUBLANES
    tab = jnp.concatenate([expert_u.reshape(E, chunks, LANES), expert_v.reshape(E, chunks, LANES)], axis=1)
    tab = tab.reshape(E * 2 * chunks, LANES)
    h3 = h.reshape(N, chunks, LANES)
    gate_t = jnp.zeros((N, SUBLANES, LANES), jnp.float32).at[:, :, :groups].set(
        gate.reshape(N, groups, SUBLANES).transpose(0, 2, 1))
    nsteps = N // tok
    slot_rows = tok * R * 2 * chunks
    kern = functools.partial(_peer_expert_kernel, tok=tok, rows=R, chunks=chunks)
    out = pl.pallas_call(
        kern,
        grid=(nsteps,),
        in_specs=[
            pl.BlockSpec((tok, R), lambda i: (0, 0), memory_space=pltpu.SMEM),
            pl.BlockSpec((tok, R), lambda i: (jnp.minimum(i + 1, nsteps - 1), 0), memory_space=pltpu.SMEM),
            pl.BlockSpec((tok, chunks, LANES), lambda i: (i, 0, 0)),
            pl.BlockSpec((tok, SUBLANES, LANES), lambda i: (i, 0, 0)),
            pl.BlockSpec(memory_space=pl.ANY),
        ],
        out_specs=pl.BlockSpec((tok, chunks, LANES), lambda i: (i, 0, 0)),
        out_shape=jax.ShapeDtypeStruct((N, chunks, LANES), jnp.float32),
        scratch_shapes=[pltpu.VMEM((2 * slot_rows, LANES), jnp.float32), pltpu.SemaphoreType.DMA((2,))],
        compiler_params=pltpu.CompilerParams(dimension_semantics=("arbitrary",),
                                             vmem_limit_bytes=48 * 1024 * 1024,
                                             disable_bounds_checks=True),
        name="peer_experts",
    )(eid, eid, h3, gate_t, tab)
    return out.reshape(N, D)


def _peer_route(h, w_pq, sub_keys):
    n_tok = h.shape[0]
    q = (h @ w_pq).reshape(n_tok, PEER_HEADS, 2, PEER_HALF)
    s = jnp.einsum('nhpc,hpkc->nhpk', q, sub_keys).astype(jnp.float32)
    s1, i1 = lax.top_k(s[:, :, 0], PEER_TOPK)
    s2, i2 = lax.top_k(s[:, :, 1], PEER_TOPK)
    cand = (s1[..., :, None] + s2[..., None, :]).reshape(n_tok, PEER_HEADS, PEER_TOPK * PEER_TOPK)
    cid = (i1[..., :, None] * PEER_KEYS + i2[..., None, :]).reshape(n_tok, PEER_HEADS, PEER_TOPK * PEER_TOPK)
    top_s, pos = lax.top_k(cand, PEER_TOPK)
    eid = jnp.take_along_axis(cid, pos, axis=-1)
    gate = jax.nn.softmax(top_s, axis=-1)
    return eid.reshape(n_tok, PEER_HEADS * PEER_TOPK), gate.reshape(n_tok, PEER_HEADS * PEER_TOPK)


def _resid_kernel(x_ref, g_ref, f_ref, o_ref):
    o_ref[...] = x_ref[...] + g_ref[...] * f_ref[...]


def _resid(x, g, f):
    B, T, D = x.shape
    tb = min(T, 512)
    return pl.pallas_call(
        _resid_kernel,
        grid=(B, T // tb),
        in_specs=[pl.BlockSpec((1, tb, D), lambda b, t: (b, t, 0)),
                  pl.BlockSpec((1, 1, D), lambda b, t: (b, 0, 0)),
                  pl.BlockSpec((1, tb, D), lambda b, t: (b, t, 0))],
        out_specs=pl.BlockSpec((1, tb, D), lambda b, t: (b, t, 0)),
        out_shape=jax.ShapeDtypeStruct(x.shape, x.dtype),
    )(x, g, f)


def _layer(x, c, k_buf, v_buf, wkv0, shift0, rel_bias, p):
    B, T, _ = x.shape
    mod = jax.nn.silu(c.astype(jnp.float32)) @ p['ada_w'] + p['ada_b']
    sh1, sc1, g1, sh2, sc2, g2 = jnp.split(mod[:, None, :], 6, axis=-1)
    h = _rms(x, p['norm1_g']) * (1.0 + sc1) + sh1
    proj = h @ p['w_in']
    q = _rms(proj[..., :C_ATTN].reshape(B, T, N_HEADS_ATTN, HEAD_DIM), p['q_norm_g'])
    k = _rms(proj[..., C_ATTN:2 * C_ATTN].reshape(B, T, N_HEADS_ATTN, HEAD_DIM), p['k_norm_g'])
    v = proj[..., 2 * C_ATTN:3 * C_ATTN].reshape(B, T, N_HEADS_ATTN, HEAD_DIM).astype(jnp.float32)
    if k_buf is None:
        o_attn = _dilated_attn_prompt(q, k, v, rel_bias)
        keep = min(MAX_WINDOW, T)
        k_rows, v_rows = k[:, T - keep:], v[:, T - keep:]
        wkv0 = jnp.zeros((B, N_HEADS_RWKV, HEAD_DIM, HEAD_DIM), jnp.float32)
        shift0 = jnp.zeros((B, COLS_RWKV), jnp.float32)
    else:
        o_attn = _dilated_attn_sample(q, k, v, k_buf, v_buf, rel_bias)
        k_rows, v_rows = k, v
    y_rwkv, s_fin, shift_new = _rwkv_mixer(
        proj[..., 3 * C_ATTN:], shift0, wkv0, p['mu_shift'], p['w0'], p['w_w2'], p['a0'], p['w_a2'],
        p['w_g2'], p['k_k'], p['k_a'], p['r_k'], p['lnx_g'], p['lnx_b'])
    mix = jnp.concatenate([o_attn.reshape(B, T, C_ATTN), y_rwkv], axis=-1) @ p['w_out']
    x = x + g1 * mix
    h2 = _rms(x, p['norm2_g']) * (1.0 + sc2) + sh2
    return x, h2, g2, k_rows, v_rows, s_fin, shift_new


def kernel(x_prompt, x_sample, c_prompt, c_sample, cache_k_win, cache_v_win, state_wkv, state_shift,
           ada_w, ada_b, norm1_g, norm2_g, w_in, q_norm_g, k_norm_g, rel_bias, mu_shift, w0, w_w2, a0,
           w_a2, w_g2, k_k, k_a, r_k, lnx_g, lnx_b, w_out, w_peer_q, peer_sub_keys, expert_u, expert_v):
    xp, xs = x_prompt, x_sample
    kp_l, vp_l, sp_l, hp_l = [], [], [], []
    ks_l, vs_l, ss_l, hs_l = [], [], [], []
    names = ('ada_w', 'ada_b', 'norm1_g', 'norm2_g', 'w_in', 'q_norm_g', 'k_norm_g', 'mu_shift', 'w0', 'w_w2',
             'a0', 'w_a2', 'w_g2', 'k_k', 'k_a', 'r_k', 'lnx_g', 'lnx_b', 'w_out', 'w_peer_q', 'peer_sub_keys',
             'expert_u', 'expert_v')
    vals = (ada_w, ada_b, norm1_g, norm2_g, w_in, q_norm_g, k_norm_g, mu_shift, w0, w_w2, a0, w_a2, w_g2, k_k,
            k_a, r_k, lnx_g, lnx_b, w_out, w_peer_q, peer_sub_keys, expert_u, expert_v)
    for l in range(DEPTH):
        p = {n: v[l] for n, v in zip(names, vals)}
        xp, h2p, g2p, kp, vp, sp, hp = _layer(xp, c_prompt, None, None, None, None, rel_bias, p)
        xs, h2s, g2s, kn, vn, sn, hn = _layer(xs, c_sample, cache_k_win[l], cache_v_win[l], state_wkv[l],
                                              state_shift[l], rel_bias, p)
        n_p = h2p.shape[0] * h2p.shape[1]
        h2 = jnp.concatenate([h2p.reshape(-1, D_MODEL), h2s.reshape(-1, D_MODEL)], axis=0)
        eid, gate = _peer_route(h2, p['w_peer_q'], p['peer_sub_keys'])
        ffn = peer_experts(h2, eid, gate, p['expert_u'], p['expert_v'])
        xp = _resid(xp, g2p, ffn[:n_p].reshape(xp.shape))
        xs = _resid(xs, g2s, ffn[n_p:].reshape(xs.shape))
        kp_l.append(kp); vp_l.append(vp); sp_l.append(sp); hp_l.append(hp)
        ks_l.append(kn); vs_l.append(vn); ss_l.append(sn); hs_l.append(hn)
    return (xp, xs, jnp.stack(kp_l), jnp.stack(vp_l), jnp.stack(sp_l), jnp.stack(hp_l),
            jnp.stack(ks_l), jnp.stack(vs_l), jnp.stack(ss_l), jnp.stack(hs_l))
```

```python
import functools
import math
import jax, jax.numpy as jnp
from jax import lax
import numpy as np
from jax.experimental import pallas as pl
from jax.experimental.pallas import tpu as pltpu

D_MODEL = 1024
DEPTH = 1
HEAD_DIM = 64
N_HEADS_ATTN = 8
N_HEADS_RWKV = 8
C_ATTN = N_HEADS_ATTN * HEAD_DIM
C_RWKV = N_HEADS_RWKV * HEAD_DIM
DILATIONS = ((128, 1), (512, 4), (2048, 16))
MAX_WINDOW = 2048
N_BUCKETS = 32
MAX_DISTANCE = 2048
LORA_DECAY = 32
LORA_ICLR = 32
LORA_GATE = 64
COLS_RWKV = 3 * C_RWKV + LORA_DECAY + LORA_ICLR + LORA_GATE
D_IN = 3 * C_ATTN + COLS_RWKV
PEER_HEADS = 8
PEER_KEYS = 128
PEER_QDIM = 256
PEER_HALF = PEER_QDIM // 2
PEER_TOPK = 16
PEER_CHUNK = 256
NORM_EPS = 1e-6
GN_EPS = 64e-5
NEG_INF = -1e30
ATTN_SCALE = HEAD_DIM ** -0.5


def _rms(x, g):
    x32 = x.astype(jnp.float32)
    return x32 * lax.rsqrt(jnp.mean(x32 * x32, axis=-1, keepdims=True) + NORM_EPS) * g


def _t5_bucket(dist):
    dist = np.asarray(dist, dtype=np.int64)
    max_exact = N_BUCKETS // 2
    safe = np.maximum(dist, 1) / max_exact
    large = max_exact + (np.log(safe) / math.log(MAX_DISTANCE / max_exact) * (N_BUCKETS - max_exact)).astype(np.int64)
    large = np.minimum(large, N_BUCKETS - 1)
    return np.where(dist < max_exact, dist, large).astype(np.int32)


def _branch_prompt(q, k, v, rel_bias, window, dil):
    B, T, H, C = q.shape
    n = window // dil
    L = T + (-T) % window
    G = L // window

    def blocks(a):
        a = jnp.pad(a, ((0, 0), (0, L - T), (0, 0), (0, 0)))
        a = a.reshape(B, L // dil, dil, H, C).transpose(0, 2, 1, 3, 4)
        return a.reshape(B, dil, G, n, H, C)

    def with_prev(a):
        prev = jnp.pad(a[:, :, :-1], ((0, 0), (0, 0), (1, 0), (0, 0), (0, 0), (0, 0)))
        return jnp.concatenate([prev, a], axis=3)

    qb = blocks(q)
    kb = with_prev(blocks(k))
    vb = with_prev(blocks(v))
    qi = np.arange(n)[:, None]
    ki = np.arange(2 * n)[None, :]
    j = n + qi - ki
    band = (j >= 0) & (j <= n)
    pad_key = (np.arange(G) == 0)[:, None, None] & (ki < n)[None]
    mask = band[None] & ~pad_key
    bias = rel_bias[_t5_bucket(np.clip(j, 0, n) * dil)].transpose(2, 0, 1).astype(jnp.float32)
    logits = jnp.einsum('brgqhc,brgshc->brghqs', qb, kb).astype(jnp.float32) * ATTN_SCALE + bias
    logits = jnp.where(mask[None, None, :, None], logits, NEG_INF)
    m = jnp.max(logits, axis=-1, keepdims=True)
    e = jnp.exp(logits - m)
    s = jnp.sum(e, axis=-1)
    o = jnp.einsum('brghqs,brgshc->brgqhc', e, vb.astype(jnp.float32)) / s.transpose(0, 1, 2, 4, 3)[..., None]
    lse = (m[..., 0] + jnp.log(s)).transpose(0, 1, 2, 4, 3)
    o = o.reshape(B, dil, L // dil, H, C).transpose(0, 2, 1, 3, 4).reshape(B, L, H, C)[:, :T]
    lse = lse.reshape(B, dil, L // dil, H).transpose(0, 2, 1, 3).reshape(B, L, H)[:, :T]
    return o, lse


def _branch_sample(q, k_all, v_all, rel_bias, window, dil, lb):
    S = q.shape[1]
    n = window // dil
    jj = np.arange(n + 1)
    idx = lb + np.arange(S)[:, None] - jj[None, :] * dil
    valid = idx >= 0
    idx_c = np.maximum(idx, 0)
    ks = k_all[:, idx_c].astype(jnp.float32)
    vs = v_all[:, idx_c].astype(jnp.float32)
    bias = rel_bias[_t5_bucket(jj * dil)].T.astype(jnp.float32)
    logits = jnp.einsum('bqhc,bqjhc->bhqj', q, ks).astype(jnp.float32) * ATTN_SCALE + bias[None, :, None, :]
    logits = jnp.where(valid[None, None], logits, NEG_INF)
    m = jnp.max(logits, axis=-1, keepdims=True)
    e = jnp.exp(logits - m)
    s = jnp.sum(e, axis=-1)
    o = jnp.einsum('bhqj,bqjhc->bqhc', e, vs) / s.transpose(0, 2, 1)[..., None]
    lse = (m[..., 0] + jnp.log(s)).transpose(0, 2, 1)
    return o, lse


def _merge_branches(outs, lses):
    wts = jax.nn.softmax(jnp.stack(lses), axis=0)
    return jnp.einsum('ibth,ibthc->bthc', wts, jnp.stack(outs))


def _dilated_attn_prompt(q, k, v, rel_bias):
    outs, lses = [], []
    for window, dil in DILATIONS:
        o, lse = _branch_prompt(q, k, v, rel_bias, window, dil)
        outs.append(o)
        lses.append(lse)
    return _merge_branches(outs, lses)


def _dilated_attn_sample(q, k_new, v_new, k_buf, v_buf, rel_bias):
    lb = k_buf.shape[1]
    k_all = jnp.concatenate([k_buf.astype(jnp.float32), k_new.astype(jnp.float32)], axis=1)
    v_all = jnp.concatenate([v_buf.astype(jnp.float32), v_new.astype(jnp.float32)], axis=1)
    outs, lses = [], []
    for window, dil in DILATIONS:
        o, lse = _branch_sample(q, k_all, v_all, rel_bias, window, dil, lb)
        outs.append(o)
        lses.append(lse)
    return _merge_branches(outs, lses)


LANES = 128
SUBLANES = 8
WKV_ROWS_PER_PASS = 4


def _wkv_kernel(kop_ref, vop_ref, s0_ref, y_ref, s_ref, *, tt, vp, kd):
    j = pl.program_id(1)
    kgs = kd // SUBLANES
    G = WKV_ROWS_PER_PASS

    @pl.when(j == 0)
    def _():
        s_ref[...] = s0_ref[...]

    def step(t, c):
        for v0 in range(0, vp, G):
            accs = [None] * G
            for kg in range(kgs):
                ks = pl.ds(kg * SUBLANES, SUBLANES)
                nkk = kop_ref[t, 1, ks, :]
                for i in range(G):
                    p = s_ref[v0 + i, ks, :] * nkk
                    accs[i] = p if accs[i] is None else accs[i] + p
            sa = [jnp.sum(a, axis=0, keepdims=True) for a in accs]
            vv = [vop_ref[t, pl.ds(v0 + i, 1), :] for i in range(G)]
            yacc = [None] * G
            for kg in range(kgs):
                ks = pl.ds(kg * SUBLANES, SUBLANES)
                w = kop_ref[t, 0, ks, :]
                b = kop_ref[t, 2, ks, :]
                k = kop_ref[t, 3, ks, :]
                r = kop_ref[t, 4, ks, :]
                for i in range(G):
                    s2 = s_ref[v0 + i, ks, :] * w + sa[i] * b + vv[i] * k
                    s_ref[v0 + i, ks, :] = s2
                    p = s2 * r
                    yacc[i] = p if yacc[i] is None else yacc[i] + p
            for i in range(G):
                y_ref[t, pl.ds(v0 + i, 1), :] = jnp.sum(yacc[i], axis=0, keepdims=True)
        return c

    lax.fori_loop(0, tt, step, 0)


def _wkv_scan(r, w, k, v, kk, a, s0):
    B, T, H, N = r.shape
    bh = B * H
    nvh = max(1, LANES // bh)
    L = nvh * bh
    assert L % LANES == 0 and N % (nvh * WKV_ROWS_PER_PASS) == 0 and N % SUBLANES == 0
    vp = N // nvh
    tt = T if T <= 32 else 32
    assert T % tt == 0

    def kform(x):
        x = jnp.transpose(x, (1, 3, 0, 2)).reshape(T, N, 1, bh)
        return jnp.broadcast_to(x, (T, N, nvh, bh)).reshape(T, N, L)

    kop = jnp.stack([kform(w), kform(-kk), kform(kk * a), kform(k), kform(r)], axis=1)
    vop = jnp.transpose(v, (1, 3, 0, 2)).reshape(T, nvh, vp, bh)
    vop = jnp.transpose(vop, (0, 2, 1, 3)).reshape(T, vp, L)
    s0t = jnp.transpose(s0, (2, 3, 0, 1)).reshape(nvh, vp, N, bh)
    s0t = jnp.transpose(s0t, (1, 2, 0, 3)).reshape(vp, N, L)
    kern = functools.partial(_wkv_kernel, tt=tt, vp=vp, kd=N)
    y, sf = pl.pallas_call(
        kern,
        grid=(L // LANES, T // tt),
        in_specs=[pl.BlockSpec((tt, 5, N, LANES), lambda l, j: (j, 0, 0, l)),
                  pl.BlockSpec((tt, vp, LANES), lambda l, j: (j, 0, l)),
                  pl.BlockSpec((vp, N, LANES), lambda l, j: (0, 0, l))],
        out_specs=[pl.BlockSpec((tt, vp, LANES), lambda l, j: (j, 0, l)),
                   pl.BlockSpec((vp, N, LANES), lambda l, j: (0, 0, l))],
        out_shape=[jax.ShapeDtypeStruct((T, vp, L), jnp.float32),
                   jax.ShapeDtypeStruct((vp, N, L), jnp.float32)],
        compiler_params=pltpu.CompilerParams(dimension_semantics=("parallel", "arbitrary"),
                                             vmem_limit_bytes=48 * 1024 * 1024),
        name="wkv_scan",
    )(kop, vop, s0t)
    y = jnp.transpose(y.reshape(T, vp, nvh, B, H), (3, 0, 4, 2, 1)).reshape(B, T, H, N)
    sf = jnp.transpose(sf.reshape(vp, N, nvh, B, H), (3, 4, 2, 0, 1)).reshape(B, H, N, N)
    return y, sf


def _rwkv_mixer(pb, shift0, s0, mu, w0, w_w2, a0, w_a2, w_g2, k_k, k_a, r_k, lnx_g, lnx_b):
    B, T, _ = pb.shape
    pb = pb.astype(jnp.float32)
    prev = jnp.concatenate([shift0[:, None, :].astype(jnp.float32), pb[:, :-1]], axis=1)
    xm = pb + (prev - pb) * mu
    c = C_RWKV
    r, k, v = xm[..., :c], xm[..., c:2 * c], xm[..., 2 * c:3 * c]
    o = 3 * c
    wl = xm[..., o:o + LORA_DECAY]
    al = xm[..., o + LORA_DECAY:o + LORA_DECAY + LORA_ICLR]
    gl = xm[..., o + LORA_DECAY + LORA_ICLR:]
    w_log = -jax.nn.softplus(-(w0 + jnp.tanh(wl) @ w_w2)) - 0.5
    decay = jnp.exp(-jnp.exp(w_log))
    a = jax.nn.sigmoid(a0 + al @ w_a2)
    g = jax.nn.sigmoid(gl) @ w_g2

    def heads(t):
        return t.reshape(B, T, N_HEADS_RWKV, HEAD_DIM)
    kk = heads(k * k_k)
    kk = kk / jnp.maximum(jnp.sqrt(jnp.sum(kk * kk, axis=-1, keepdims=True)), 1e-12)
    k = k * (1.0 + (a - 1.0) * k_a)
    r_h, k_h, v_h = heads(r), heads(k), heads(v)
    y, s_fin = _wkv_scan(r_h, heads(decay), k_h, v_h, kk, heads(a), s0.astype(jnp.float32))
    mean = jnp.mean(y, axis=-1, keepdims=True)
    var = jnp.mean(jnp.square(y - mean), axis=-1, keepdims=True)
    y = ((y - mean) * lax.rsqrt(var + GN_EPS)).reshape(B, T, C_RWKV) * lnx_g + lnx_b
    bonus = jnp.sum(r_h * k_h * r_k, axis=-1, keepdims=True) * v_h
    y = (y + bonus.reshape(B, T, C_RWKV)) * g
    return y, s_fin, pb[:, -1]


def _gelu_exact(x):
    return 0.5 * x * (1.0 + lax.erf(x * (2.0 ** -0.5)))


def _peer_expert_kernel(eid0_ref, eidn_ref, h_ref, gate_ref, tab_ref, out_ref, buf, sem, *, tok, rows, chunks):
    i = pl.program_id(0)
    n = pl.num_programs(0)
    erow = 2 * chunks
    slot_rows = tok * rows * erow
    groups = rows // SUBLANES
    slot = i % 2

    def row_copy(eid_ref, t, r, dst_slot):
        e = eid_ref[t, r]
        src = tab_ref.at[pl.ds(pl.multiple_of(e * erow, erow), erow)]
        dst = buf.at[pl.ds(pl.multiple_of(dst_slot * slot_rows + (t * rows + r) * erow, erow), erow)]
        return pltpu.make_async_copy(src, dst, sem.at[dst_slot])

    def wait_slot(s):
        pltpu.make_async_copy(tab_ref.at[pl.ds(0, slot_rows)],
                              buf.at[pl.ds(pl.multiple_of(s * slot_rows, slot_rows), slot_rows)],
                              sem.at[s]).wait()

    @pl.when(i == 0)
    def _():
        def prime(t, c):
            for r in range(rows):
                row_copy(eid0_ref, t, r, 0).start()
            return c
        lax.fori_loop(0, tok, prime, 0)

    wait_slot(slot)
    lane = lax.broadcasted_iota(jnp.int32, (SUBLANES, LANES), 1)

    def token(t, c):
        for r in range(rows):
            row_copy(eidn_ref, t, r, 1 - slot).start()
        base = slot * slot_rows + t * (rows * erow)
        hb = [jnp.broadcast_to(h_ref[t, pl.ds(k, 1), :], (SUBLANES, LANES)) for k in range(chunks)]
        s_tile = jnp.zeros((SUBLANES, LANES), jnp.float32)
        for g in range(groups):
            acc = None
            for k in range(chunks):
                u = buf[pl.ds(base + g * SUBLANES * erow + k, SUBLANES, stride=erow), :]
                p = u * hb[k]
                acc = p if acc is None else acc + p
            col = jnp.sum(acc, axis=1, keepdims=True)
            s_tile = jnp.where(lane == g, col, s_tile)
        w_tile = _gelu_exact(s_tile) * gate_ref[t]
        accs = [None] * chunks
        for g in range(groups):
            wg = jnp.broadcast_to(w_tile[:, g:g + 1], (SUBLANES, LANES))
            for k in range(chunks):
                v = buf[pl.ds(base + g * SUBLANES * erow + chunks + k, SUBLANES, stride=erow), :]
                p = wg * v
                accs[k] = p if accs[k] is None else accs[k] + p
        out_ref[t] = jnp.concatenate([jnp.sum(a, axis=0, keepdims=True) for a in accs], axis=0)
        return c

    lax.fori_loop(0, tok, token, 0)

    @pl.when(i == n - 1)
    def _():
        wait_slot(1 - slot)


def peer_experts(h, eid, gate, expert_u, expert_v, *, tok=8):
    N, D = h.shape
    R = eid.shape[1]
    E = expert_u.shape[0]
    chunks = D // LANES
    assert N % tok == 0 and R % SUBLANES == 0 and D % LANES == 0
    groups = R // SUBLANES
    tab = jnp.concatenate([expert_u.reshape(E, chunks, LANES), expert_v.reshape(E, chunks, LANES)], axis=1)
    tab = tab.reshape(E * 2 * chunks, LANES)
    h3 = h.reshape(N, chunks, LANES)
    gate_t = jnp.zeros((N, SUBLANES, LANES), jnp.float32).at[:, :, :groups].set(
        gate.reshape(N, groups, SUBLANES).transpose(0, 2, 1))
    nsteps = N // tok
    slot_rows = tok * R * 2 * chunks
    kern = functools.partial(_peer_expert_kernel, tok=tok, rows=R, chunks=chunks)
    out = pl.pallas_call(
        kern,
        grid=(nsteps,),
        in_specs=[
            pl.BlockSpec((tok, R), lambda i: (0, 0), memory_space=pltpu.SMEM),
            pl.BlockSpec((tok, R), lambda i: (jnp.minimum(i + 1, nsteps - 1), 0), memory_space=pltpu.SMEM),
            pl.BlockSpec((tok, chunks, LANES), lambda i: (i, 0, 0)),
            pl.BlockSpec((tok, SUBLANES, LANES), lambda i: (i, 0, 0)),
            pl.BlockSpec(memory_space=pl.ANY),
        ],
        out_specs=pl.BlockSpec((tok, chunks, LANES), lambda i: (i, 0, 0)),
        out_shape=jax.ShapeDtypeStruct((N, chunks, LANES), jnp.float32),
        scratch_shapes=[pltpu.VMEM((2 * slot_rows, LANES), jnp.float32), pltpu.SemaphoreType.DMA((2,))],
        compiler_params=pltpu.CompilerParams(dimension_semantics=("arbitrary",),
                                             vmem_limit_bytes=48 * 1024 * 1024,
                                             disable_bounds_checks=True),
        name="peer_experts",
    )(eid, eid, h3, gate_t, tab)
    return out.reshape(N, D)


PEER_PAIRS = tuple((a, b) for a in range(PEER_TOPK) for b in range(PEER_TOPK) if (a + 1) * (b + 1) <= PEER_TOPK)


def _top_rows(x, pos, k):
    vals, idxs = [], []
    for _ in range(k):
        m = jnp.max(x, axis=0, keepdims=True)
        i = jnp.min(jnp.where(x == m, pos, jnp.float32(1e9)), axis=0, keepdims=True)
        vals.append(m)
        idxs.append(i)
        x = jnp.where(pos == i, -jnp.inf, x)
    return vals, idxs


def _route_kernel(s_ref, eid_ref, gate_ref, *, heads, keys, topk):
    tb = s_ref.shape[1]
    key_pos = lax.broadcasted_iota(jnp.int32, (keys, tb), 0).astype(jnp.float32)
    nrow = -(-len(PEER_PAIRS) // SUBLANES) * SUBLANES
    pair_pos = lax.broadcasted_iota(jnp.int32, (nrow, tb), 0).astype(jnp.float32)
    for h in range(heads):
        v1, i1 = _top_rows(s_ref[pl.ds((2 * h) * keys, keys), :], key_pos, topk)
        v2, i2 = _top_rows(s_ref[pl.ds((2 * h + 1) * keys, keys), :], key_pos, topk)
        cand = jnp.full((nrow, tb), -jnp.inf, jnp.float32)
        cid = jnp.zeros((nrow, tb), jnp.float32)
        for p, (a, b) in enumerate(PEER_PAIRS):
            cand = jnp.where(pair_pos == p, v1[a] + v2[b], cand)
            cid = jnp.where(pair_pos == p, i1[a] * keys + i2[b], cid)
        top_s, top_p = _top_rows(cand, pair_pos, topk)
        es = [jnp.exp(s - top_s[0]) for s in top_s]
        den = es[0]
        for e in es[1:]:
            den = den + e
        for r in range(topk):
            eid = jnp.sum(jnp.where(pair_pos == top_p[r], cid, 0.0), axis=0, keepdims=True)
            eid_ref[pl.ds(h * topk + r, 1), :] = eid.astype(jnp.int32)
            gate_ref[pl.ds(h * topk + r, 1), :] = es[r] / den


def _peer_topk(s_t, *, tb=LANES):
    rows, n = s_t.shape
    assert rows == PEER_HEADS * 2 * PEER_KEYS and n % tb == 0
    kern = functools.partial(_route_kernel, heads=PEER_HEADS, keys=PEER_KEYS, topk=PEER_TOPK)
    return pl.pallas_call(
        kern,
        grid=(n // tb,),
        in_specs=[pl.BlockSpec((rows, tb), lambda i: (0, i))],
        out_specs=[pl.BlockSpec((PEER_HEADS * PEER_TOPK, tb), lambda i: (0, i)),
                   pl.BlockSpec((PEER_HEADS * PEER_TOPK, tb), lambda i: (0, i))],
        out_shape=[jax.ShapeDtypeStruct((PEER_HEADS * PEER_TOPK, n), jnp.int32),
                   jax.ShapeDtypeStruct((PEER_HEADS * PEER_TOPK, n), jnp.float32)],
        compiler_params=pltpu.CompilerParams(dimension_semantics=("parallel",)),
        name="peer_topk",
    )(s_t)


def _peer_route(h, w_pq, sub_keys):
    n_tok = h.shape[0]
    q = (h @ w_pq).reshape(n_tok, PEER_HEADS, 2, PEER_HALF)
    s_t = jnp.einsum('nhpc,hpkc->hpkn', q, sub_keys).astype(jnp.float32)
    eid_t, gate_t = _peer_topk(s_t.reshape(PEER_HEADS * 2 * PEER_KEYS, n_tok))
    return eid_t.T, gate_t.T


def _resid_kernel(x_ref, g_ref, f_ref, o_ref):
    o_ref[...] = x_ref[...] + g_ref[...] * f_ref[...]


def _resid(x, g, f):
    B, T, D = x.shape
    tb = min(T, 512)
    return pl.pallas_call(
        _resid_kernel,
        grid=(B, T // tb),
        in_specs=[pl.BlockSpec((1, tb, D), lambda b, t: (b, t, 0)),
                  pl.BlockSpec((1, 1, D), lambda b, t: (b, 0, 0)),
                  pl.BlockSpec((1, tb, D), lambda b, t: (b, t, 0))],
        out_specs=pl.BlockSpec((1, tb, D), lambda b, t: (b, t, 0)),
        out_shape=jax.ShapeDtypeStruct(x.shape, x.dtype),
        name="ffn_residual",
    )(x, g, f)


def _layer(x, c, k_buf, v_buf, wkv0, shift0, rel_bias, p):
    B, T, _ = x.shape
    mod = jax.nn.silu(c.astype(jnp.float32)) @ p['ada_w'] + p['ada_b']
    sh1, sc1, g1, sh2, sc2, g2 = jnp.split(mod[:, None, :], 6, axis=-1)
    h = _rms(x, p['norm1_g']) * (1.0 + sc1) + sh1
    proj = h @ p['w_in']
    q = _rms(proj[..., :C_ATTN].reshape(B, T, N_HEADS_ATTN, HEAD_DIM), p['q_norm_g'])
    k = _rms(proj[..., C_ATTN:2 * C_ATTN].reshape(B, T, N_HEADS_ATTN, HEAD_DIM), p['k_norm_g'])
    v = proj[..., 2 * C_ATTN:3 * C_ATTN].reshape(B, T, N_HEADS_ATTN, HEAD_DIM).astype(jnp.float32)
    if k_buf is None:
        o_attn = _dilated_attn_prompt(q, k, v, rel_bias)
        keep = min(MAX_WINDOW, T)
        k_rows, v_rows = k[:, T - keep:], v[:, T - keep:]
        wkv0 = jnp.zeros((B, N_HEADS_RWKV, HEAD_DIM, HEAD_DIM), jnp.float32)
        shift0 = jnp.zeros((B, COLS_RWKV), jnp.float32)
    else:
        o_attn = _dilated_attn_sample(q, k, v, k_buf, v_buf, rel_bias)
        k_rows, v_rows = k, v
    y_rwkv, s_fin, shift_new = _rwkv_mixer(
        proj[..., 3 * C_ATTN:], shift0, wkv0, p['mu_shift'], p['w0'], p['w_w2'], p['a0'], p['w_a2'],
        p['w_g2'], p['k_k'], p['k_a'], p['r_k'], p['lnx_g'], p['lnx_b'])
    mix = jnp.concatenate([o_attn.reshape(B, T, C_ATTN), y_rwkv], axis=-1) @ p['w_out']
    x = x + g1 * mix
    h2 = _rms(x, p['norm2_g']) * (1.0 + sc2) + sh2
    return x, h2, g2, k_rows, v_rows, s_fin, shift_new


def kernel(x_prompt, x_sample, c_prompt, c_sample, cache_k_win, cache_v_win, state_wkv, state_shift,
           ada_w, ada_b, norm1_g, norm2_g, w_in, q_norm_g, k_norm_g, rel_bias, mu_shift, w0, w_w2, a0,
           w_a2, w_g2, k_k, k_a, r_k, lnx_g, lnx_b, w_out, w_peer_q, peer_sub_keys, expert_u, expert_v):
    xp, xs = x_prompt, x_sample
    kp_l, vp_l, sp_l, hp_l = [], [], [], []
    ks_l, vs_l, ss_l, hs_l = [], [], [], []
    names = ('ada_w', 'ada_b', 'norm1_g', 'norm2_g', 'w_in', 'q_norm_g', 'k_norm_g', 'mu_shift', 'w0', 'w_w2',
             'a0', 'w_a2', 'w_g2', 'k_k', 'k_a', 'r_k', 'lnx_g', 'lnx_b', 'w_out', 'w_peer_q', 'peer_sub_keys',
             'expert_u', 'expert_v')
    vals = (ada_w, ada_b, norm1_g, norm2_g, w_in, q_norm_g, k_norm_g, mu_shift, w0, w_w2, a0, w_a2, w_g2, k_k,
            k_a, r_k, lnx_g, lnx_b, w_out, w_peer_q, peer_sub_keys, expert_u, expert_v)
    for l in range(DEPTH):
        p = {n: v[l] for n, v in zip(names, vals)}
        xp, h2p, g2p, kp, vp, sp, hp = _layer(xp, c_prompt, None, None, None, None, rel_bias, p)
        xs, h2s, g2s, kn, vn, sn, hn = _layer(xs, c_sample, cache_k_win[l], cache_v_win[l], state_wkv[l],
                                              state_shift[l], rel_bias, p)
        n_p = h2p.shape[0] * h2p.shape[1]
        h2 = jnp.concatenate([h2p.reshape(-1, D_MODEL), h2s.reshape(-1, D_MODEL)], axis=0)
        eid, gate = _peer_route(h2, p['w_peer_q'], p['peer_sub_keys'])
        ffn = peer_experts(h2, eid, gate, p['expert_u'], p['expert_v'])
        xp = _resid(xp, g2p, ffn[:n_p].reshape(xp.shape))
        xs = _resid(xs, g2s, ffn[n_p:].reshape(xs.shape))
        kp_l.append(kp); vp_l.append(vp); sp_l.append(sp); hp_l.append(hp)
        ks_l.append(kn); vs_l.append(vn); ss_l.append(sn); hs_l.append(hn)
    return (xp, xs, jnp.stack(kp_l), jnp.stack(vp_l), jnp.stack(sp_l), jnp.stack(hp_l),
            jnp.stack(ks_l), jnp.stack(vs_l), jnp.stack(ss_l), jnp.stack(hs_l))
```

```python
import functools
import math
import jax, jax.numpy as jnp
from jax import lax
import numpy as np
from jax.experimental import pallas as pl
from jax.experimental.pallas import tpu as pltpu

D_MODEL = 1024
DEPTH = 1
HEAD_DIM = 64
N_HEADS_ATTN = 8
N_HEADS_RWKV = 8
C_ATTN = N_HEADS_ATTN * HEAD_DIM
C_RWKV = N_HEADS_RWKV * HEAD_DIM
DILATIONS = ((128, 1), (512, 4), (2048, 16))
MAX_WINDOW = 2048
N_BUCKETS = 32
MAX_DISTANCE = 2048
LORA_DECAY = 32
LORA_ICLR = 32
LORA_GATE = 64
COLS_RWKV = 3 * C_RWKV + LORA_DECAY + LORA_ICLR + LORA_GATE
D_IN = 3 * C_ATTN + COLS_RWKV
PEER_HEADS = 8
PEER_KEYS = 128
PEER_QDIM = 256
PEER_HALF = PEER_QDIM // 2
PEER_TOPK = 16
PEER_CHUNK = 256
NORM_EPS = 1e-6
GN_EPS = 64e-5
NEG_INF = -1e30
ATTN_SCALE = HEAD_DIM ** -0.5


def _rms(x, g):
    x32 = x.astype(jnp.float32)
    return x32 * lax.rsqrt(jnp.mean(x32 * x32, axis=-1, keepdims=True) + NORM_EPS) * g


def _t5_bucket(dist):
    dist = np.asarray(dist, dtype=np.int64)
    max_exact = N_BUCKETS // 2
    safe = np.maximum(dist, 1) / max_exact
    large = max_exact + (np.log(safe) / math.log(MAX_DISTANCE / max_exact) * (N_BUCKETS - max_exact)).astype(np.int64)
    large = np.minimum(large, N_BUCKETS - 1)
    return np.where(dist < max_exact, dist, large).astype(np.int32)


def _branch_prompt(q, k, v, rel_bias, window, dil):
    B, T, H, C = q.shape
    n = window // dil
    L = T + (-T) % window
    G = L // window

    def blocks(a):
        a = jnp.pad(a, ((0, 0), (0, L - T), (0, 0), (0, 0)))
        a = a.reshape(B, L // dil, dil, H, C).transpose(0, 2, 1, 3, 4)
        return a.reshape(B, dil, G, n, H, C)

    def with_prev(a):
        prev = jnp.pad(a[:, :, :-1], ((0, 0), (0, 0), (1, 0), (0, 0), (0, 0), (0, 0)))
        return jnp.concatenate([prev, a], axis=3)

    qb = blocks(q)
    kb = with_prev(blocks(k))
    vb = with_prev(blocks(v))
    qi = np.arange(n)[:, None]
    ki = np.arange(2 * n)[None, :]
    j = n + qi - ki
    band = (j >= 0) & (j <= n)
    pad_key = (np.arange(G) == 0)[:, None, None] & (ki < n)[None]
    mask = band[None] & ~pad_key
    bias = rel_bias[_t5_bucket(np.clip(j, 0, n) * dil)].transpose(2, 0, 1).astype(jnp.float32)
    logits = jnp.einsum('brgqhc,brgshc->brghqs', qb, kb).astype(jnp.float32) * ATTN_SCALE + bias
    logits = jnp.where(mask[None, None, :, None], logits, NEG_INF)
    m = jnp.max(logits, axis=-1, keepdims=True)
    e = jnp.exp(logits - m)
    s = jnp.sum(e, axis=-1)
    o = jnp.einsum('brghqs,brgshc->brgqhc', e, vb.astype(jnp.float32)) / s.transpose(0, 1, 2, 4, 3)[..., None]
    lse = (m[..., 0] + jnp.log(s)).transpose(0, 1, 2, 4, 3)
    o = o.reshape(B, dil, L // dil, H, C).transpose(0, 2, 1, 3, 4).reshape(B, L, H, C)[:, :T]
    lse = lse.reshape(B, dil, L // dil, H).transpose(0, 2, 1, 3).reshape(B, L, H)[:, :T]
    return o, lse


def _branch_sample(q, k_all, v_all, rel_bias, window, dil, lb):
    S = q.shape[1]
    n = window // dil
    jj = np.arange(n + 1)
    idx = lb + np.arange(S)[:, None] - jj[None, :] * dil
    valid = idx >= 0
    idx_c = np.maximum(idx, 0)
    ks = k_all[:, idx_c].astype(jnp.float32)
    vs = v_all[:, idx_c].astype(jnp.float32)
    bias = rel_bias[_t5_bucket(jj * dil)].T.astype(jnp.float32)
    logits = jnp.einsum('bqhc,bqjhc->bhqj', q, ks).astype(jnp.float32) * ATTN_SCALE + bias[None, :, None, :]
    logits = jnp.where(valid[None, None], logits, NEG_INF)
    m = jnp.max(logits, axis=-1, keepdims=True)
    e = jnp.exp(logits - m)
    s = jnp.sum(e, axis=-1)
    o = jnp.einsum('bhqj,bqjhc->bqhc', e, vs) / s.transpose(0, 2, 1)[..., None]
    lse = (m[..., 0] + jnp.log(s)).transpose(0, 2, 1)
    return o, lse


def _merge_branches(outs, lses):
    wts = jax.nn.softmax(jnp.stack(lses), axis=0)
    return jnp.einsum('ibth,ibthc->bthc', wts, jnp.stack(outs))


def _dilated_attn_prompt(q, k, v, rel_bias):
    outs, lses = [], []
    for window, dil in DILATIONS:
        o, lse = _branch_prompt(q, k, v, rel_bias, window, dil)
        outs.append(o)
        lses.append(lse)
    return _merge_branches(outs, lses)


def _dilated_attn_sample(q, k_new, v_new, k_buf, v_buf, rel_bias):
    lb = k_buf.shape[1]
    k_all = jnp.concatenate([k_buf.astype(jnp.float32), k_new.astype(jnp.float32)], axis=1)
    v_all = jnp.concatenate([v_buf.astype(jnp.float32), v_new.astype(jnp.float32)], axis=1)
    outs, lses = [], []
    for window, dil in DILATIONS:
        o, lse = _branch_sample(q, k_all, v_all, rel_bias, window, dil, lb)
        outs.append(o)
        lses.append(lse)
    return _merge_branches(outs, lses)


LANES = 128
SUBLANES = 8
WKV_ROWS_PER_PASS = 4


def _wkv_kernel(kop_ref, vop_ref, s0_ref, y_ref, s_ref, *, tt, vp, kd):
    j = pl.program_id(1)
    kgs = kd // SUBLANES
    G = WKV_ROWS_PER_PASS

    @pl.when(j == 0)
    def _():
        s_ref[...] = s0_ref[...]

    def step(t, c):
        for v0 in range(0, vp, G):
            accs = [None] * G
            for kg in range(kgs):
                ks = pl.ds(kg * SUBLANES, SUBLANES)
                nkk = kop_ref[t, 1, ks, :]
                for i in range(G):
                    p = s_ref[v0 + i, ks, :] * nkk
                    accs[i] = p if accs[i] is None else accs[i] + p
            sa = [jnp.sum(a, axis=0, keepdims=True) for a in accs]
            vv = [vop_ref[t, pl.ds(v0 + i, 1), :] for i in range(G)]
            yacc = [None] * G
            for kg in range(kgs):
                ks = pl.ds(kg * SUBLANES, SUBLANES)
                w = kop_ref[t, 0, ks, :]
                b = kop_ref[t, 2, ks, :]
                k = kop_ref[t, 3, ks, :]
                r = kop_ref[t, 4, ks, :]
                for i in range(G):
                    s2 = s_ref[v0 + i, ks, :] * w + sa[i] * b + vv[i] * k
                    s_ref[v0 + i, ks, :] = s2
                    p = s2 * r
                    yacc[i] = p if yacc[i] is None else yacc[i] + p
            for i in range(G):
                y_ref[t, pl.ds(v0 + i, 1), :] = jnp.sum(yacc[i], axis=0, keepdims=True)
        return c

    lax.fori_loop(0, tt, step, 0)


def _wkv_scan(r, w, k, v, kk, a, s0):
    B, T, H, N = r.shape
    bh = B * H
    nvh = max(1, LANES // bh)
    L = nvh * bh
    assert L % LANES == 0 and N % (nvh * WKV_ROWS_PER_PASS) == 0 and N % SUBLANES == 0
    vp = N // nvh
    tt = T if T <= 32 else 32
    assert T % tt == 0

    def kform(x):
        x = jnp.transpose(x, (1, 3, 0, 2)).reshape(T, N, 1, bh)
        return jnp.broadcast_to(x, (T, N, nvh, bh)).reshape(T, N, L)

    kop = jnp.stack([kform(w), kform(-kk), kform(kk * a), kform(k), kform(r)], axis=1)
    vop = jnp.transpose(v, (1, 3, 0, 2)).reshape(T, nvh, vp, bh)
    vop = jnp.transpose(vop, (0, 2, 1, 3)).reshape(T, vp, L)
    s0t = jnp.transpose(s0, (2, 3, 0, 1)).reshape(nvh, vp, N, bh)
    s0t = jnp.transpose(s0t, (1, 2, 0, 3)).reshape(vp, N, L)
    kern = functools.partial(_wkv_kernel, tt=tt, vp=vp, kd=N)
    y, sf = pl.pallas_call(
        kern,
        grid=(L // LANES, T // tt),
        in_specs=[pl.BlockSpec((tt, 5, N, LANES), lambda l, j: (j, 0, 0, l)),
                  pl.BlockSpec((tt, vp, LANES), lambda l, j: (j, 0, l)),
                  pl.BlockSpec((vp, N, LANES), lambda l, j: (0, 0, l))],
        out_specs=[pl.BlockSpec((tt, vp, LANES), lambda l, j: (j, 0, l)),
                   pl.BlockSpec((vp, N, LANES), lambda l, j: (0, 0, l))],
        out_shape=[jax.ShapeDtypeStruct((T, vp, L), jnp.float32),
                   jax.ShapeDtypeStruct((vp, N, L), jnp.float32)],
        compiler_params=pltpu.CompilerParams(dimension_semantics=("parallel", "arbitrary"),
                                             vmem_limit_bytes=48 * 1024 * 1024),
        name="wkv_scan",
    )(kop, vop, s0t)
    y = jnp.transpose(y.reshape(T, vp, nvh, B, H), (3, 0, 4, 2, 1)).reshape(B, T, H, N)
    sf = jnp.transpose(sf.reshape(vp, N, nvh, B, H), (3, 4, 2, 0, 1)).reshape(B, H, N, N)
    return y, sf


def _rwkv_mixer(pb, shift0, s0, mu, w0, w_w2, a0, w_a2, w_g2, k_k, k_a, r_k, lnx_g, lnx_b):
    B, T, _ = pb.shape
    pb = pb.astype(jnp.float32)
    prev = jnp.concatenate([shift0[:, None, :].astype(jnp.float32), pb[:, :-1]], axis=1)
    xm = pb + (prev - pb) * mu
    c = C_RWKV
    r, k, v = xm[..., :c], xm[..., c:2 * c], xm[..., 2 * c:3 * c]
    o = 3 * c
    wl = xm[..., o:o + LORA_DECAY]
    al = xm[..., o + LORA_DECAY:o + LORA_DECAY + LORA_ICLR]
    gl = xm[..., o + LORA_DECAY + LORA_ICLR:]
    w_log = -jax.nn.softplus(-(w0 + jnp.tanh(wl) @ w_w2)) - 0.5
    decay = jnp.exp(-jnp.exp(w_log))
    a = jax.nn.sigmoid(a0 + al @ w_a2)
    g = jax.nn.sigmoid(gl) @ w_g2

    def heads(t):
        return t.reshape(B, T, N_HEADS_RWKV, HEAD_DIM)
    kk = heads(k * k_k)
    kk = kk / jnp.maximum(jnp.sqrt(jnp.sum(kk * kk, axis=-1, keepdims=True)), 1e-12)
    k = k * (1.0 + (a - 1.0) * k_a)
    r_h, k_h, v_h = heads(r), heads(k), heads(v)
    y, s_fin = _wkv_scan(r_h, heads(decay), k_h, v_h, kk, heads(a), s0.astype(jnp.float32))
    mean = jnp.mean(y, axis=-1, keepdims=True)
    var = jnp.mean(jnp.square(y - mean), axis=-1, keepdims=True)
    y = ((y - mean) * lax.rsqrt(var + GN_EPS)).reshape(B, T, C_RWKV) * lnx_g + lnx_b
    bonus = jnp.sum(r_h * k_h * r_k, axis=-1, keepdims=True) * v_h
    y = (y + bonus.reshape(B, T, C_RWKV)) * g
    return y, s_fin, pb[:, -1]


def _gelu_exact(x):
    return 0.5 * x * (1.0 + lax.erf(x * (2.0 ** -0.5)))


PEER_TOK = 8


def _peer_expert_kernel(eidc_ref, eidn_ref, h_ref, gate_ref, tab_ref, out_ref, buf, sem, *, rows, dim):
    i = pl.program_id(0)
    n = pl.num_programs(0)
    tok = PEER_TOK
    chunks = dim // LANES
    erow = 2 * chunks
    groups = rows // SUBLANES
    slot_groups = tok * groups

    def row_copy(eid_ref, src_t, t, r, slot):
        e = eid_ref[src_t, r]
        src = tab_ref.at[pl.ds(pl.multiple_of(e * erow, erow), erow)]
        dst = buf.at[slot * slot_groups + t * groups + r // SUBLANES, :, r % SUBLANES, :]
        return pltpu.make_async_copy(src, dst, sem.at[slot])

    def wait_slot(slot):
        region = buf.at[pl.ds(slot * slot_groups, slot_groups)]
        pltpu.make_async_copy(region, region, sem.at[slot]).wait()

    def compute(t_blk, t, slot):
        g0 = slot * slot_groups + t * groups
        hb = [jnp.broadcast_to(h_ref[t_blk:t_blk + 1, k * LANES:(k + 1) * LANES], (SUBLANES, LANES))
              for k in range(chunks)]
        lane = lax.broadcasted_iota(jnp.int32, (SUBLANES, LANES), 1)
        s_tile = jnp.zeros((SUBLANES, LANES), jnp.float32)
        for g in range(groups):
            acc = None
            for k in range(chunks):
                p = buf[g0 + g, k] * hb[k]
                acc = p if acc is None else acc + p
            s_tile = jnp.where(lane == g, jnp.sum(acc, axis=1, keepdims=True), s_tile)
        w_tile = _gelu_exact(s_tile) * gate_ref[t_blk]
        accs = [None] * chunks
        for g in range(groups):
            wg = jnp.broadcast_to(w_tile[:, g:g + 1], (SUBLANES, LANES))
            for k in range(chunks):
                p = wg * buf[g0 + g, chunks + k]
                accs[k] = p if accs[k] is None else accs[k] + p
        out_ref[t_blk:t_blk + 1, :] = jnp.concatenate([jnp.sum(a, axis=0, keepdims=True) for a in accs], axis=1)

    @pl.when(i == 0)
    def _():
        for t in range(tok):
            for r in range(rows):
                row_copy(eidc_ref, t, t, r, 0).start()

    wait_slot(0)
    for t in range(tok):
        for r in range(rows):
            row_copy(eidc_ref, tok + t, t, r, 1).start()
        compute(t, t, 0)
    wait_slot(1)
    for t in range(tok):
        for r in range(rows):
            row_copy(eidn_ref, t, t, r, 0).start()
        compute(tok + t, t, 1)

    @pl.when(i == n - 1)
    def _():
        wait_slot(0)


def peer_experts(h, eid, gate_tile, expert_u, expert_v):
    N, D = h.shape
    R = eid.shape[1]
    E = expert_u.shape[0]
    chunks = D // LANES
    step_tok = 2 * PEER_TOK
    assert N % step_tok == 0 and R % SUBLANES == 0 and D % LANES == 0
    groups = R // SUBLANES
    tab = jnp.concatenate([expert_u.reshape(E, chunks, LANES), expert_v.reshape(E, chunks, LANES)], axis=1)
    tab = tab.reshape(E * 2 * chunks, LANES)
    nsteps = N // step_tok
    kern = functools.partial(_peer_expert_kernel, rows=R, dim=D)
    return pl.pallas_call(
        kern,
        grid=(nsteps,),
        in_specs=[
            pl.BlockSpec((step_tok, R), lambda i: (i, 0), memory_space=pltpu.SMEM),
            pl.BlockSpec((step_tok, R), lambda i: (jnp.minimum(i + 1, nsteps - 1), 0), memory_space=pltpu.SMEM),
            pl.BlockSpec((step_tok, D), lambda i: (i, 0)),
            pl.BlockSpec((step_tok, SUBLANES, LANES), lambda i: (i, 0, 0)),
            pl.BlockSpec(memory_space=pl.ANY),
        ],
        out_specs=pl.BlockSpec((step_tok, D), lambda i: (i, 0)),
        out_shape=jax.ShapeDtypeStruct((N, D), jnp.float32),
        scratch_shapes=[pltpu.VMEM((2 * PEER_TOK * groups, 2 * chunks, SUBLANES, LANES), jnp.float32),
                        pltpu.SemaphoreType.DMA((2,))],
        compiler_params=pltpu.CompilerParams(dimension_semantics=("arbitrary",),
                                             vmem_limit_bytes=48 * 1024 * 1024,
                                             disable_bounds_checks=True),
        name="peer_experts",
    )(eid, eid, h, gate_tile, tab)


PEER_PAIRS = tuple((a, b) for a in range(PEER_TOPK) for b in range(PEER_TOPK) if (a + 1) * (b + 1) <= PEER_TOPK)


def _top_rows(x, pos, k):
    vals, idxs = [], []
    for _ in range(k):
        m = jnp.max(x, axis=0, keepdims=True)
        i = jnp.min(jnp.where(x == m, pos, jnp.float32(1e9)), axis=0, keepdims=True)
        vals.append(m)
        idxs.append(i)
        x = jnp.where(pos == i, -jnp.inf, x)
    return vals, idxs


def _route_kernel(s_ref, eid_ref, gate_ref, *, heads, keys, topk):
    tb = s_ref.shape[1]
    key_pos = lax.broadcasted_iota(jnp.int32, (keys, tb), 0).astype(jnp.float32)
    nrow = -(-len(PEER_PAIRS) // SUBLANES) * SUBLANES
    pair_pos = lax.broadcasted_iota(jnp.int32, (nrow, tb), 0).astype(jnp.float32)
    for h in range(heads):
        v1, i1 = _top_rows(s_ref[pl.ds((2 * h) * keys, keys), :], key_pos, topk)
        v2, i2 = _top_rows(s_ref[pl.ds((2 * h + 1) * keys, keys), :], key_pos, topk)
        cand = jnp.full((nrow, tb), -jnp.inf, jnp.float32)
        cid = jnp.zeros((nrow, tb), jnp.float32)
        for p, (a, b) in enumerate(PEER_PAIRS):
            cand = jnp.where(pair_pos == p, v1[a] + v2[b], cand)
            cid = jnp.where(pair_pos == p, i1[a] * keys + i2[b], cid)
        top_s, top_p = _top_rows(cand, pair_pos, topk)
        es = [jnp.exp(s - top_s[0]) for s in top_s]
        den = es[0]
        for e in es[1:]:
            den = den + e
        for r in range(topk):
            eid = jnp.sum(jnp.where(pair_pos == top_p[r], cid, 0.0), axis=0, keepdims=True)
            eid_ref[pl.ds(h * topk + r, 1), :] = eid.astype(jnp.int32)
            gate_ref[pl.ds(h * topk + r, 1), :] = es[r] / den


def _peer_topk(s_t, *, tb=LANES):
    rows, n = s_t.shape
    assert rows == PEER_HEADS * 2 * PEER_KEYS and n % tb == 0
    kern = functools.partial(_route_kernel, heads=PEER_HEADS, keys=PEER_KEYS, topk=PEER_TOPK)
    return pl.pallas_call(
        kern,
        grid=(n // tb,),
        in_specs=[pl.BlockSpec((rows, tb), lambda i: (0, i))],
        out_specs=[pl.BlockSpec((PEER_HEADS * PEER_TOPK, tb), lambda i: (0, i)),
                   pl.BlockSpec((PEER_HEADS * PEER_TOPK, tb), lambda i: (0, i))],
        out_shape=[jax.ShapeDtypeStruct((PEER_HEADS * PEER_TOPK, n), jnp.int32),
                   jax.ShapeDtypeStruct((PEER_HEADS * PEER_TOPK, n), jnp.float32)],
        compiler_params=pltpu.CompilerParams(dimension_semantics=("parallel",)),
        name="peer_topk",
    )(s_t)


def _peer_route(h, w_pq, sub_keys):
    n_tok = h.shape[0]
    q = (h @ w_pq).reshape(n_tok, PEER_HEADS, 2, PEER_HALF)
    s_t = jnp.einsum('nhpc,hpkc->hpkn', q, sub_keys).astype(jnp.float32)
    eid_t, gate_t = _peer_topk(s_t.reshape(PEER_HEADS * 2 * PEER_KEYS, n_tok))
    groups = PEER_HEADS * PEER_TOPK // SUBLANES
    gate_tile = jnp.transpose(gate_t.reshape(groups, SUBLANES, n_tok), (2, 1, 0))
    gate_tile = jnp.pad(gate_tile, ((0, 0), (0, 0), (0, LANES - groups)))
    return eid_t.T, gate_tile


def _resid_kernel(x_ref, g_ref, f_ref, o_ref):
    o_ref[...] = x_ref[...] + g_ref[...] * f_ref[...]


def _resid(x, g, f):
    B, T, D = x.shape
    tb = min(T, 512)
    return pl.pallas_call(
        _resid_kernel,
        grid=(B, T // tb),
        in_specs=[pl.BlockSpec((1, tb, D), lambda b, t: (b, t, 0)),
                  pl.BlockSpec((1, 1, D), lambda b, t: (b, 0, 0)),
                  pl.BlockSpec((1, tb, D), lambda b, t: (b, t, 0))],
        out_specs=pl.BlockSpec((1, tb, D), lambda b, t: (b, t, 0)),
        out_shape=jax.ShapeDtypeStruct(x.shape, x.dtype),
        name="ffn_residual",
    )(x, g, f)


def _layer(x, c, k_buf, v_buf, wkv0, shift0, rel_bias, p):
    B, T, _ = x.shape
    mod = jax.nn.silu(c.astype(jnp.float32)) @ p['ada_w'] + p['ada_b']
    sh1, sc1, g1, sh2, sc2, g2 = jnp.split(mod[:, None, :], 6, axis=-1)
    h = _rms(x, p['norm1_g']) * (1.0 + sc1) + sh1
    proj = h @ p['w_in']
    q = _rms(proj[..., :C_ATTN].reshape(B, T, N_HEADS_ATTN, HEAD_DIM), p['q_norm_g'])
    k = _rms(proj[..., C_ATTN:2 * C_ATTN].reshape(B, T, N_HEADS_ATTN, HEAD_DIM), p['k_norm_g'])
    v = proj[..., 2 * C_ATTN:3 * C_ATTN].reshape(B, T, N_HEADS_ATTN, HEAD_DIM).astype(jnp.float32)
    if k_buf is None:
        o_attn = _dilated_attn_prompt(q, k, v, rel_bias)
        keep = min(MAX_WINDOW, T)
        k_rows, v_rows = k[:, T - keep:], v[:, T - keep:]
        wkv0 = jnp.zeros((B, N_HEADS_RWKV, HEAD_DIM, HEAD_DIM), jnp.float32)
        shift0 = jnp.zeros((B, COLS_RWKV), jnp.float32)
    else:
        o_attn = _dilated_attn_sample(q, k, v, k_buf, v_buf, rel_bias)
        k_rows, v_rows = k, v
    y_rwkv, s_fin, shift_new = _rwkv_mixer(
        proj[..., 3 * C_ATTN:], shift0, wkv0, p['mu_shift'], p['w0'], p['w_w2'], p['a0'], p['w_a2'],
        p['w_g2'], p['k_k'], p['k_a'], p['r_k'], p['lnx_g'], p['lnx_b'])
    mix = jnp.concatenate([o_attn.reshape(B, T, C_ATTN), y_rwkv], axis=-1) @ p['w_out']
    x = x + g1 * mix
    h2 = _rms(x, p['norm2_g']) * (1.0 + sc2) + sh2
    return x, h2, g2, k_rows, v_rows, s_fin, shift_new


def kernel(x_prompt, x_sample, c_prompt, c_sample, cache_k_win, cache_v_win, state_wkv, state_shift,
           ada_w, ada_b, norm1_g, norm2_g, w_in, q_norm_g, k_norm_g, rel_bias, mu_shift, w0, w_w2, a0,
           w_a2, w_g2, k_k, k_a, r_k, lnx_g, lnx_b, w_out, w_peer_q, peer_sub_keys, expert_u, expert_v):
    xp, xs = x_prompt, x_sample
    kp_l, vp_l, sp_l, hp_l = [], [], [], []
    ks_l, vs_l, ss_l, hs_l = [], [], [], []
    names = ('ada_w', 'ada_b', 'norm1_g', 'norm2_g', 'w_in', 'q_norm_g', 'k_norm_g', 'mu_shift', 'w0', 'w_w2',
             'a0', 'w_a2', 'w_g2', 'k_k', 'k_a', 'r_k', 'lnx_g', 'lnx_b', 'w_out', 'w_peer_q', 'peer_sub_keys',
             'expert_u', 'expert_v')
    vals = (ada_w, ada_b, norm1_g, norm2_g, w_in, q_norm_g, k_norm_g, mu_shift, w0, w_w2, a0, w_a2, w_g2, k_k,
            k_a, r_k, lnx_g, lnx_b, w_out, w_peer_q, peer_sub_keys, expert_u, expert_v)
    for l in range(DEPTH):
        p = {n: v[l] for n, v in zip(names, vals)}
        xp, h2p, g2p, kp, vp, sp, hp = _layer(xp, c_prompt, None, None, None, None, rel_bias, p)
        xs, h2s, g2s, kn, vn, sn, hn = _layer(xs, c_sample, cache_k_win[l], cache_v_win[l], state_wkv[l],
                                              state_shift[l], rel_bias, p)
        n_p = h2p.shape[0] * h2p.shape[1]
        h2 = jnp.concatenate([h2p.reshape(-1, D_MODEL), h2s.reshape(-1, D_MODEL)], axis=0)
        eid, gate_tile = _peer_route(h2, p['w_peer_q'], p['peer_sub_keys'])
        ffn = peer_experts(h2, eid, gate_tile, p['expert_u'], p['expert_v'])
        xp = _resid(xp, g2p, ffn[:n_p].reshape(xp.shape))
        xs = _resid(xs, g2s, ffn[n_p:].reshape(xs.shape))
        kp_l.append(kp); vp_l.append(vp); sp_l.append(sp); hp_l.append(hp)
        ks_l.append(kn); vs_l.append(vn); ss_l.append(sn); hs_l.append(hn)
    return (xp, xs, jnp.stack(kp_l), jnp.stack(vp_l), jnp.stack(sp_l), jnp.stack(hp_l),
            jnp.stack(ks_l), jnp.stack(vs_l), jnp.stack(ss_l), jnp.stack(hs_l))
```

```python
import functools
import math
import jax, jax.numpy as jnp
from jax import lax
import numpy as np
from jax.experimental import pallas as pl
from jax.experimental.pallas import tpu as pltpu

D_MODEL = 1024
DEPTH = 1
HEAD_DIM = 64
N_HEADS_ATTN = 8
N_HEADS_RWKV = 8
C_ATTN = N_HEADS_ATTN * HEAD_DIM
C_RWKV = N_HEADS_RWKV * HEAD_DIM
DILATIONS = ((128, 1), (512, 4), (2048, 16))
MAX_WINDOW = 2048
N_BUCKETS = 32
MAX_DISTANCE = 2048
LORA_DECAY = 32
LORA_ICLR = 32
LORA_GATE = 64
COLS_RWKV = 3 * C_RWKV + LORA_DECAY + LORA_ICLR + LORA_GATE
D_IN = 3 * C_ATTN + COLS_RWKV
PEER_HEADS = 8
PEER_KEYS = 128
PEER_QDIM = 256
PEER_HALF = PEER_QDIM // 2
PEER_TOPK = 16
PEER_CHUNK = 256
NORM_EPS = 1e-6
GN_EPS = 64e-5
NEG_INF = -1e30
ATTN_SCALE = HEAD_DIM ** -0.5


def _rms(x, g):
    x32 = x.astype(jnp.float32)
    return x32 * lax.rsqrt(jnp.mean(x32 * x32, axis=-1, keepdims=True) + NORM_EPS) * g


def _t5_bucket(dist):
    dist = np.asarray(dist, dtype=np.int64)
    max_exact = N_BUCKETS // 2
    safe = np.maximum(dist, 1) / max_exact
    large = max_exact + (np.log(safe) / math.log(MAX_DISTANCE / max_exact) * (N_BUCKETS - max_exact)).astype(np.int64)
    large = np.minimum(large, N_BUCKETS - 1)
    return np.where(dist < max_exact, dist, large).astype(np.int32)


LANES = 128
SUBLANES = 8


def _sample_attn_kernel(q_ref, kn_ref, vn_ref, k16_ref, v16_ref, k4_ref, v4_ref, bc_ref, bn_ref, o_ref):
    _, S, H, C = q_ref.shape
    n = k4_ref.shape[1]
    kn = kn_ref[0]
    vn = vn_ref[0]
    recent = n - n // 4
    for s in range(S):
        q = q_ref[0, s][None]
        dn = jnp.sum(kn * q, axis=-1, keepdims=True) * ATTN_SCALE
        outs, lses = [], []
        for i in range(3):
            if i == 0:
                kc = k4_ref[0, recent:n].reshape(n, H, C)
                vc = v4_ref[0, recent:n].reshape(n, H, C)
            elif i == 1:
                kc, vc = k4_ref[0, :, s], v4_ref[0, :, s]
            else:
                kc, vc = k16_ref[0, :, s], v16_ref[0, :, s]
            lc = jnp.sum(kc * q, axis=-1, keepdims=True) * ATTN_SCALE + bc_ref[i, s]
            ln = dn + bn_ref[i, s]
            m = jnp.maximum(jnp.max(lc, axis=0), jnp.max(ln, axis=0))
            ec = jnp.exp(lc - m)
            en = jnp.exp(ln - m)
            den = jnp.sum(ec, axis=0) + jnp.sum(en, axis=0)
            outs.append((jnp.sum(ec * vc, axis=0) + jnp.sum(en * vn, axis=0)) / den)
            lses.append(m + jnp.log(den))
        mm = jnp.maximum(jnp.maximum(lses[0], lses[1]), lses[2])
        ws = [jnp.exp(l - mm) for l in lses]
        wsum = ws[0] + ws[1] + ws[2]
        o_ref[0, s] = (ws[0] / wsum) * outs[0] + (ws[1] / wsum) * outs[1] + (ws[2] / wsum) * outs[2]


def _sample_bias_tables(rel_bias, S, lb, n):
    jj_c = np.zeros((3, S, n), np.int64)
    ok_c = np.zeros((3, S, n), bool)
    jj_n = np.zeros((3, S, S), np.int64)
    ok_n = np.zeros((3, S, S), bool)
    p = np.arange(n)
    for i, (window, dil) in enumerate(DILATIONS):
        for s in range(S):
            if dil == 1:
                pos = lb - n + p
            else:
                pos = (lb - n * dil) + dil * p + s
            d = lb + s - pos
            ok_c[i, s] = (d % dil == 0) & (d >= dil) & (d <= window)
            jj_c[i, s] = np.clip(d // dil, 0, window // dil)
            dn = s - np.arange(S)
            ok_n[i, s] = (dn >= 0) & (dn % dil == 0) & (dn <= window)
            jj_n[i, s] = np.clip(dn // dil, 0, window // dil)

    def table(jj, ok):
        dil = np.array([d for _, d in DILATIONS]).reshape(3, 1, 1)
        b = rel_bias[_t5_bucket(jj * dil)].astype(jnp.float32)
        return jnp.where(ok[..., None], b, NEG_INF)[..., None]
    return table(jj_c, ok_c), table(jj_n, ok_n)


def _sample_attn(q, k_new, v_new, k_buf, v_buf, rel_bias):
    B, S, H, C = q.shape
    lb = k_buf.shape[1]
    n = DILATIONS[0][0]
    assert DILATIONS == ((n, 1), (4 * n, 4), (16 * n, 16)) and lb == 16 * n and S <= 4
    bc, bn = _sample_bias_tables(rel_bias, S, lb, n)
    k16 = k_buf.reshape(B, n, 16, H, C)
    v16 = v_buf.reshape(B, n, 16, H, C)
    k4 = k_buf.reshape(B, lb // 4, 4, H, C)
    v4 = v_buf.reshape(B, lb // 4, 4, H, C)
    tok = pl.BlockSpec((1, S, H, C), lambda b: (b, 0, 0, 0))
    far = pl.BlockSpec((1, n, S, H, C), lambda b: (b, 0, 0, 0, 0))
    near = pl.BlockSpec((1, n, 4, H, C), lambda b: (b, lb // 4 // n - 1, 0, 0, 0))
    return pl.pallas_call(
        _sample_attn_kernel,
        grid=(B,),
        in_specs=[tok, tok, tok, far, far, near, near,
                  pl.BlockSpec(bc.shape, lambda b: (0, 0, 0, 0, 0)),
                  pl.BlockSpec(bn.shape, lambda b: (0, 0, 0, 0, 0))],
        out_specs=tok,
        out_shape=jax.ShapeDtypeStruct((B, S, H, C), jnp.float32),
        compiler_params=pltpu.CompilerParams(dimension_semantics=("parallel",),
                                             vmem_limit_bytes=48 * 1024 * 1024),
        name="sample_attn",
    )(q, k_new, v_new, k16, v16, k4, v4, bc, bn)


def _prompt_attn_kernel(q_ref, kp_ref, kc_ref, vp_ref, vc_ref, bias_ref, o_ref, lse_ref):
    g = pl.program_id(2)
    n = q_ref.shape[1]
    heads = bias_ref.shape[0]
    lane = lax.broadcasted_iota(jnp.int32, (n, LANES), 1)
    nt = (((1,), (1,)), ((), ()))
    per = LANES // HEAD_DIM
    sls = [slice(hp * LANES, (hp + 1) * LANES) for hp in range(heads // per)]
    keeps = [(lane >= half * HEAD_DIM) & (lane < (half + 1) * HEAD_DIM) for half in range(per)]
    logits = []
    for h in range(heads):
        sl, keep = sls[h // per], keeps[h % per]
        qh = jnp.where(keep, q_ref[0, :, sl], 0.0)
        lp = lax.dot_general(qh, kp_ref[0, :, sl], nt, preferred_element_type=jnp.float32) * ATTN_SCALE + bias_ref[h, :, :n]
        lc = lax.dot_general(qh, kc_ref[0, :, sl], nt, preferred_element_type=jnp.float32) * ATTN_SCALE + bias_ref[h, :, n:]
        logits.append((jnp.where(g == 0, NEG_INF, lp), lc))
    probs = []
    for lp, lc in logits:
        m = jnp.max(jnp.maximum(lp, lc), axis=-1, keepdims=True)
        ep = jnp.exp(lp - m)
        ec = jnp.exp(lc - m)
        s = jnp.sum(ep + ec, axis=-1, keepdims=True)
        probs.append((ep, ec, s, m + jnp.log(s)))
    for hp, sl in enumerate(sls):
        o2 = jnp.zeros((n, LANES), jnp.float32)
        l2 = jnp.zeros((n, LANES), jnp.float32)
        for half in range(per):
            ep, ec, s, lse = probs[hp * per + half]
            o = (jnp.dot(ep, vp_ref[0, :, sl], preferred_element_type=jnp.float32)
                 + jnp.dot(ec, vc_ref[0, :, sl], preferred_element_type=jnp.float32)) / s
            o2 = jnp.where(keeps[half], o, o2)
            l2 = jnp.where(keeps[half], lse, l2)
        o_ref[0, :, sl] = o2
        lse_ref[0, :, sl] = l2


def _merge_kernel(o1, o2, o3, l1, l2, l3, out):
    m = jnp.maximum(jnp.maximum(l1[...], l2[...]), l3[...])
    w1, w2, w3 = jnp.exp(l1[...] - m), jnp.exp(l2[...] - m), jnp.exp(l3[...] - m)
    ws = w1 + w2 + w3
    out[...] = (w1 / ws) * o1[...] + (w2 / ws) * o2[...] + (w3 / ws) * o3[...]


def _prompt_bias(rel_bias, n, dil):
    qi = np.arange(n)[:, None]
    ki = np.arange(2 * n)[None, :]
    j = n + qi - ki
    band = (j >= 0) & (j <= n)
    b = rel_bias[_t5_bucket(np.clip(j, 0, n) * dil)].transpose(2, 0, 1).astype(jnp.float32)
    return jnp.where(band[None], b, NEG_INF)


def _prompt_attn(q, k, v, rel_bias):
    B, T, HC = q.shape
    H = HC // HEAD_DIM
    outs, lses = [], []
    for window, dil in DILATIONS:
        n = window // dil
        assert T % window == 0 and HC % LANES == 0
        G = T // window
        view = lambda a: a.reshape(B, T // dil, dil * HC)
        cur = pl.BlockSpec((1, n, HC), lambda b, r, g: (b, g, r))
        prev = pl.BlockSpec((1, n, HC), lambda b, r, g: (b, jnp.maximum(g - 1, 0), r))
        bias = _prompt_bias(rel_bias, n, dil)
        o, lse = pl.pallas_call(
            _prompt_attn_kernel,
            grid=(B, dil, G),
            in_specs=[cur, prev, cur, prev, cur, pl.BlockSpec((H, n, 2 * n), lambda b, r, g: (0, 0, 0))],
            out_specs=[cur, cur],
            out_shape=[jax.ShapeDtypeStruct((B, T // dil, dil * HC), jnp.float32)] * 2,
            compiler_params=pltpu.CompilerParams(dimension_semantics=("parallel", "parallel", "arbitrary")),
            name="prompt_attn_d%d" % dil,
        )(view(q), view(k), view(k), view(v), view(v), bias)
        outs.append(o.reshape(B, T, HC))
        lses.append(lse.reshape(B, T, HC))
    tb = 512
    blk = pl.BlockSpec((1, tb, HC), lambda b, t: (b, t, 0))
    return pl.pallas_call(
        _merge_kernel, grid=(B, T // tb), in_specs=[blk] * 6, out_specs=blk,
        out_shape=jax.ShapeDtypeStruct((B, T, HC), jnp.float32),
        compiler_params=pltpu.CompilerParams(dimension_semantics=("parallel", "parallel")),
        name="prompt_attn_merge",
    )(*outs, *lses)


WKV_ROWS_PER_PASS = 4


def _wkv_kernel(w_ref, nkk_ref, b_ref, k_ref, r_ref, vop_ref, s0_ref, y_ref, s_ref, *, tt, vp, kd):
    j = pl.program_id(1)
    kgs = kd // SUBLANES
    G = WKV_ROWS_PER_PASS

    @pl.when(j == 0)
    def _():
        s_ref[...] = s0_ref[...]

    def step(t, c):
        for v0 in range(0, vp, G):
            accs = [None] * G
            for kg in range(kgs):
                ks = pl.ds(kg * SUBLANES, SUBLANES)
                nkk = nkk_ref[t, ks, :]
                for i in range(G):
                    p = s_ref[v0 + i, ks, :] * nkk
                    accs[i] = p if accs[i] is None else accs[i] + p
            sa = [jnp.sum(a, axis=0, keepdims=True) for a in accs]
            vv = [vop_ref[t, pl.ds(v0 + i, 1), :] for i in range(G)]
            yacc = [None] * G
            for kg in range(kgs):
                ks = pl.ds(kg * SUBLANES, SUBLANES)
                w = w_ref[t, ks, :]
                b = b_ref[t, ks, :]
                k = k_ref[t, ks, :]
                r = r_ref[t, ks, :]
                for i in range(G):
                    s2 = s_ref[v0 + i, ks, :] * w + sa[i] * b + vv[i] * k
                    s_ref[v0 + i, ks, :] = s2
                    p = s2 * r
                    yacc[i] = p if yacc[i] is None else yacc[i] + p
            for i in range(G):
                y_ref[t, pl.ds(v0 + i, 1), :] = jnp.sum(yacc[i], axis=0, keepdims=True)
        return c

    lax.fori_loop(0, tt, step, 0)


def _wkv_scan(r, w, k, v, kk, a, s0):
    B, T, H, N = r.shape
    bh = B * H
    nvh = max(1, LANES // bh)
    L = nvh * bh
    assert L % LANES == 0 and N % (nvh * WKV_ROWS_PER_PASS) == 0 and N % SUBLANES == 0
    vp = N // nvh
    tt = T if T <= 32 else 32
    assert T % tt == 0

    def kform(x):
        x = jnp.transpose(x, (1, 3, 0, 2)).reshape(T, N, 1, bh)
        return jnp.broadcast_to(x, (T, N, nvh, bh)).reshape(T, N, L)

    kops = [kform(w), kform(-kk), kform(kk * a), kform(k), kform(r)]
    vop = jnp.transpose(v, (1, 3, 0, 2)).reshape(T, nvh, vp, bh)
    vop = jnp.transpose(vop, (0, 2, 1, 3)).reshape(T, vp, L)
    s0t = jnp.transpose(s0, (2, 3, 0, 1)).reshape(nvh, vp, N, bh)
    s0t = jnp.transpose(s0t, (1, 2, 0, 3)).reshape(vp, N, L)
    kern = functools.partial(_wkv_kernel, tt=tt, vp=vp, kd=N)
    y, sf = pl.pallas_call(
        kern,
        grid=(L // LANES, T // tt),
        in_specs=[pl.BlockSpec((tt, N, LANES), lambda l, j: (j, 0, l))] * 5 + [
                  pl.BlockSpec((tt, vp, LANES), lambda l, j: (j, 0, l)),
                  pl.BlockSpec((vp, N, LANES), lambda l, j: (0, 0, l))],
        out_specs=[pl.BlockSpec((tt, vp, LANES), lambda l, j: (j, 0, l)),
                   pl.BlockSpec((vp, N, LANES), lambda l, j: (0, 0, l))],
        out_shape=[jax.ShapeDtypeStruct((T, vp, L), jnp.float32),
                   jax.ShapeDtypeStruct((vp, N, L), jnp.float32)],
        compiler_params=pltpu.CompilerParams(dimension_semantics=("parallel", "arbitrary"),
                                             vmem_limit_bytes=48 * 1024 * 1024),
        name="wkv_scan",
    )(*kops, vop, s0t)
    y = jnp.transpose(y.reshape(T, vp, nvh, B, H), (3, 0, 4, 2, 1)).reshape(B, T, H, N)
    sf = jnp.transpose(sf.reshape(vp, N, nvh, B, H), (3, 4, 2, 0, 1)).reshape(B, H, N, N)
    return y, sf


def _rwkv_mixer(pb, shift0, s0, mu, w0, w_w2, a0, w_a2, w_g2, k_k, k_a, r_k, lnx_g, lnx_b):
    B, T, _ = pb.shape
    pb = pb.astype(jnp.float32)
    prev = jnp.concatenate([shift0[:, None, :].astype(jnp.float32), pb[:, :-1]], axis=1)
    xm = pb + (prev - pb) * mu
    c = C_RWKV
    r, k, v = xm[..., :c], xm[..., c:2 * c], xm[..., 2 * c:3 * c]
    o = 3 * c
    wl = xm[..., o:o + LORA_DECAY]
    al = xm[..., o + LORA_DECAY:o + LORA_DECAY + LORA_ICLR]
    gl = xm[..., o + LORA_DECAY + LORA_ICLR:]
    w_log = -jax.nn.softplus(-(w0 + jnp.tanh(wl) @ w_w2)) - 0.5
    decay = jnp.exp(-jnp.exp(w_log))
    a = jax.nn.sigmoid(a0 + al @ w_a2)
    g = jax.nn.sigmoid(gl) @ w_g2

    def heads(t):
        return t.reshape(B, T, N_HEADS_RWKV, HEAD_DIM)
    kk = heads(k * k_k)
    kk = kk / jnp.maximum(jnp.sqrt(jnp.sum(kk * kk, axis=-1, keepdims=True)), 1e-12)
    k = k * (1.0 + (a - 1.0) * k_a)
    r_h, k_h, v_h = heads(r), heads(k), heads(v)
    y, s_fin = _wkv_scan(r_h, heads(decay), k_h, v_h, kk, heads(a), s0.astype(jnp.float32))
    mean = jnp.mean(y, axis=-1, keepdims=True)
    var = jnp.mean(jnp.square(y - mean), axis=-1, keepdims=True)
    y = ((y - mean) * lax.rsqrt(var + GN_EPS)).reshape(B, T, C_RWKV) * lnx_g + lnx_b
    bonus = jnp.sum(r_h * k_h * r_k, axis=-1, keepdims=True) * v_h
    y = (y + bonus.reshape(B, T, C_RWKV)) * g
    return y, s_fin, pb[:, -1]


def _gelu_exact(x):
    return 0.5 * x * (1.0 + lax.erf(x * (2.0 ** -0.5)))


PEER_TOK = 8


def _peer_expert_kernel(eidc_ref, eidn_ref, h_ref, gate_ref, tab_ref, out_ref, buf, sem, *, rows, dim):
    i = pl.program_id(0)
    n = pl.num_programs(0)
    tok = PEER_TOK
    chunks = dim // LANES
    erow = chunks
    groups = rows // SUBLANES
    slot_groups = tok * groups

    def row_copy(eid_ref, src_t, t, r, slot):
        e = eid_ref[src_t, r]
        src = tab_ref.at[pl.ds(pl.multiple_of(e * erow, erow), erow)]
        dst = buf.at[slot * slot_groups + t * groups + r // SUBLANES, :, r % SUBLANES, :]
        return pltpu.make_async_copy(src, dst, sem.at[slot])

    def wait_slot(slot):
        region = buf.at[pl.ds(slot * slot_groups, slot_groups)]
        pltpu.make_async_copy(region, region, sem.at[slot]).wait()

    def compute(t_blk, t, slot):
        g0 = slot * slot_groups + t * groups
        hb = [jnp.broadcast_to(h_ref[t_blk:t_blk + 1, k * LANES:(k + 1) * LANES], (SUBLANES, LANES))
              for k in range(chunks)]
        lane = lax.broadcasted_iota(jnp.int32, (SUBLANES, LANES), 1)
        hi_mask = jnp.uint32(0xFFFF0000)
        s_tile = jnp.zeros((SUBLANES, LANES), jnp.float32)
        for g in range(groups):
            acc = None
            for k in range(chunks):
                p = pltpu.bitcast(buf[g0 + g, k] & hi_mask, jnp.float32) * hb[k]
                acc = p if acc is None else acc + p
            s_tile = jnp.where(lane == g, jnp.sum(acc, axis=1, keepdims=True), s_tile)
        w_tile = _gelu_exact(s_tile) * gate_ref[t_blk]
        accs = [None] * chunks
        for g in range(groups):
            wg = jnp.broadcast_to(w_tile[:, g:g + 1], (SUBLANES, LANES))
            for k in range(chunks):
                p = wg * pltpu.bitcast(buf[g0 + g, k] << 16, jnp.float32)
                accs[k] = p if accs[k] is None else accs[k] + p
        out_ref[t_blk:t_blk + 1, :] = jnp.concatenate([jnp.sum(a, axis=0, keepdims=True) for a in accs], axis=1)

    @pl.when(i == 0)
    def _():
        for t in range(tok):
            for r in range(rows):
                row_copy(eidc_ref, t, t, r, 0).start()

    wait_slot(0)
    for t in range(tok):
        for r in range(rows):
            row_copy(eidc_ref, tok + t, t, r, 1).start()
        compute(t, t, 0)
    wait_slot(1)
    for t in range(tok):
        for r in range(rows):
            row_copy(eidn_ref, t, t, r, 0).start()
        compute(tok + t, t, 1)

    @pl.when(i == n - 1)
    def _():
        wait_slot(0)


def peer_experts(h, eid, gate_tile, expert_u, expert_v):
    N, D = h.shape
    R = eid.shape[1]
    E = expert_u.shape[0]
    chunks = D // LANES
    step_tok = 2 * PEER_TOK
    assert N % step_tok == 0 and R % SUBLANES == 0 and D % LANES == 0
    groups = R // SUBLANES

    def bf16_bits(x):
        return lax.bitcast_convert_type(x.astype(jnp.bfloat16), jnp.uint16).astype(jnp.uint32)
    tab = ((bf16_bits(expert_u) << 16) | bf16_bits(expert_v)).reshape(E * chunks, LANES)
    nsteps = N // step_tok
    kern = functools.partial(_peer_expert_kernel, rows=R, dim=D)
    return pl.pallas_call(
        kern,
        grid=(nsteps,),
        in_specs=[
            pl.BlockSpec((step_tok, R), lambda i: (i, 0), memory_space=pltpu.SMEM),
            pl.BlockSpec((step_tok, R), lambda i: (jnp.minimum(i + 1, nsteps - 1), 0), memory_space=pltpu.SMEM),
            pl.BlockSpec((step_tok, D), lambda i: (i, 0)),
            pl.BlockSpec((step_tok, SUBLANES, LANES), lambda i: (i, 0, 0)),
            pl.BlockSpec(memory_space=pl.ANY),
        ],
        out_specs=pl.BlockSpec((step_tok, D), lambda i: (i, 0)),
        out_shape=jax.ShapeDtypeStruct((N, D), jnp.float32),
        scratch_shapes=[pltpu.VMEM((2 * PEER_TOK * groups, chunks, SUBLANES, LANES), jnp.uint32),
                        pltpu.SemaphoreType.DMA((2,))],
        compiler_params=pltpu.CompilerParams(dimension_semantics=("arbitrary",),
                                             vmem_limit_bytes=48 * 1024 * 1024,
                                             disable_bounds_checks=True),
        name="peer_experts",
    )(eid, eid, h, gate_tile, tab)


PEER_PAIRS = tuple((a, b) for a in range(PEER_TOPK) for b in range(PEER_TOPK) if (a + 1) * (b + 1) <= PEER_TOPK)


def _top_rows(x, pos, k):
    vals, idxs = [], []
    for _ in range(k):
        m = jnp.max(x, axis=0, keepdims=True)
        i = jnp.min(jnp.where(x == m, pos, jnp.float32(1e9)), axis=0, keepdims=True)
        vals.append(m)
        idxs.append(i)
        x = jnp.where(pos == i, -jnp.inf, x)
    return vals, idxs


def _route_kernel(s_ref, eid_ref, gate_ref, *, heads, keys, topk):
    tb = s_ref.shape[1]
    key_pos = lax.broadcasted_iota(jnp.int32, (keys, tb), 0).astype(jnp.float32)
    nrow = -(-len(PEER_PAIRS) // SUBLANES) * SUBLANES
    pair_pos = lax.broadcasted_iota(jnp.int32, (nrow, tb), 0).astype(jnp.float32)
    for h in range(heads):
        v1, i1 = _top_rows(s_ref[pl.ds((2 * h) * keys, keys), :], key_pos, topk)
        v2, i2 = _top_rows(s_ref[pl.ds((2 * h + 1) * keys, keys), :], key_pos, topk)
        cand = jnp.full((nrow, tb), -jnp.inf, jnp.float32)
        cid = jnp.zeros((nrow, tb), jnp.float32)
        for p, (a, b) in enumerate(PEER_PAIRS):
            cand = jnp.where(pair_pos == p, v1[a] + v2[b], cand)
            cid = jnp.where(pair_pos == p, i1[a] * keys + i2[b], cid)
        top_s, top_p = _top_rows(cand, pair_pos, topk)
        es = [jnp.exp(s - top_s[0]) for s in top_s]
        den = es[0]
        for e in es[1:]:
            den = den + e
        for r in range(topk):
            eid = jnp.sum(jnp.where(pair_pos == top_p[r], cid, 0.0), axis=0, keepdims=True)
            eid_ref[pl.ds(h * topk + r, 1), :] = eid.astype(jnp.int32)
            gate_ref[pl.ds(h * topk + r, 1), :] = es[r] / den


def _peer_topk(s_t, *, tb=LANES):
    rows, n = s_t.shape
    assert rows == PEER_HEADS * 2 * PEER_KEYS and n % tb == 0
    kern = functools.partial(_route_kernel, heads=PEER_HEADS, keys=PEER_KEYS, topk=PEER_TOPK)
    return pl.pallas_call(
        kern,
        grid=(n // tb,),
        in_specs=[pl.BlockSpec((rows, tb), lambda i: (0, i))],
        out_specs=[pl.BlockSpec((PEER_HEADS * PEER_TOPK, tb), lambda i: (0, i)),
                   pl.BlockSpec((PEER_HEADS * PEER_TOPK, tb), lambda i: (0, i))],
        out_shape=[jax.ShapeDtypeStruct((PEER_HEADS * PEER_TOPK, n), jnp.int32),
                   jax.ShapeDtypeStruct((PEER_HEADS * PEER_TOPK, n), jnp.float32)],
        compiler_params=pltpu.CompilerParams(dimension_semantics=("parallel",)),
        name="peer_topk",
    )(s_t)


def _peer_route(h, w_pq, sub_keys):
    n_tok = h.shape[0]
    q = (h @ w_pq).reshape(n_tok, PEER_HEADS, 2, PEER_HALF)
    s_t = jnp.einsum('nhpc,hpkc->hpkn', q, sub_keys).astype(jnp.float32)
    eid_t, gate_t = _peer_topk(s_t.reshape(PEER_HEADS * 2 * PEER_KEYS, n_tok))
    groups = PEER_HEADS * PEER_TOPK // SUBLANES
    gate_tile = jnp.transpose(gate_t.reshape(groups, SUBLANES, n_tok), (2, 1, 0))
    gate_tile = jnp.pad(gate_tile, ((0, 0), (0, 0), (0, LANES - groups)))
    return eid_t.T, gate_tile


def _resid_kernel(x_ref, g_ref, f_ref, o_ref):
    o_ref[...] = x_ref[...] + g_ref[...] * f_ref[...]


def _resid(x, g, f):
    B, T, D = x.shape
    tb = min(T, 512)
    return pl.pallas_call(
        _resid_kernel,
        grid=(B, T // tb),
        in_specs=[pl.BlockSpec((1, tb, D), lambda b, t: (b, t, 0)),
                  pl.BlockSpec((1, 1, D), lambda b, t: (b, 0, 0)),
                  pl.BlockSpec((1, tb, D), lambda b, t: (b, t, 0))],
        out_specs=pl.BlockSpec((1, tb, D), lambda b, t: (b, t, 0)),
        out_shape=jax.ShapeDtypeStruct(x.shape, x.dtype),
        name="ffn_residual",
    )(x, g, f)


def _layer(x, c, k_buf, v_buf, wkv0, shift0, rel_bias, p):
    B, T, _ = x.shape
    mod = jax.nn.silu(c.astype(jnp.float32)) @ p['ada_w'] + p['ada_b']
    sh1, sc1, g1, sh2, sc2, g2 = jnp.split(mod[:, None, :], 6, axis=-1)
    h = _rms(x, p['norm1_g']) * (1.0 + sc1) + sh1
    proj = h @ p['w_in']
    q = _rms(proj[..., :C_ATTN].reshape(B, T, N_HEADS_ATTN, HEAD_DIM), p['q_norm_g'])
    k = _rms(proj[..., C_ATTN:2 * C_ATTN].reshape(B, T, N_HEADS_ATTN, HEAD_DIM), p['k_norm_g'])
    v = proj[..., 2 * C_ATTN:3 * C_ATTN].reshape(B, T, N_HEADS_ATTN, HEAD_DIM).astype(jnp.float32)
    if k_buf is None:
        o_attn = _prompt_attn(q.reshape(B, T, C_ATTN), k.reshape(B, T, C_ATTN), v.reshape(B, T, C_ATTN), rel_bias)
        keep = min(MAX_WINDOW, T)
        k_rows, v_rows = k[:, T - keep:], v[:, T - keep:]
        wkv0 = jnp.zeros((B, N_HEADS_RWKV, HEAD_DIM, HEAD_DIM), jnp.float32)
        shift0 = jnp.zeros((B, COLS_RWKV), jnp.float32)
    else:
        o_attn = _sample_attn(q, k, v, k_buf, v_buf, rel_bias)
        k_rows, v_rows = k, v
    y_rwkv, s_fin, shift_new = _rwkv_mixer(
        proj[..., 3 * C_ATTN:], shift0, wkv0, p['mu_shift'], p['w0'], p['w_w2'], p['a0'], p['w_a2'],
        p['w_g2'], p['k_k'], p['k_a'], p['r_k'], p['lnx_g'], p['lnx_b'])
    mix = jnp.concatenate([o_attn.reshape(B, T, C_ATTN), y_rwkv], axis=-1) @ p['w_out']
    x = x + g1 * mix
    h2 = _rms(x, p['norm2_g']) * (1.0 + sc2) + sh2
    return x, h2, g2, k_rows, v_rows, s_fin, shift_new


def kernel(x_prompt, x_sample, c_prompt, c_sample, cache_k_win, cache_v_win, state_wkv, state_shift,
           ada_w, ada_b, norm1_g, norm2_g, w_in, q_norm_g, k_norm_g, rel_bias, mu_shift, w0, w_w2, a0,
           w_a2, w_g2, k_k, k_a, r_k, lnx_g, lnx_b, w_out, w_peer_q, peer_sub_keys, expert_u, expert_v):
    xp, xs = x_prompt, x_sample
    kp_l, vp_l, sp_l, hp_l = [], [], [], []
    ks_l, vs_l, ss_l, hs_l = [], [], [], []
    names = ('ada_w', 'ada_b', 'norm1_g', 'norm2_g', 'w_in', 'q_norm_g', 'k_norm_g', 'mu_shift', 'w0', 'w_w2',
             'a0', 'w_a2', 'w_g2', 'k_k', 'k_a', 'r_k', 'lnx_g', 'lnx_b', 'w_out', 'w_peer_q', 'peer_sub_keys',
             'expert_u', 'expert_v')
    vals = (ada_w, ada_b, norm1_g, norm2_g, w_in, q_norm_g, k_norm_g, mu_shift, w0, w_w2, a0, w_a2, w_g2, k_k,
            k_a, r_k, lnx_g, lnx_b, w_out, w_peer_q, peer_sub_keys, expert_u, expert_v)
    for l in range(DEPTH):
        p = {n: v[l] for n, v in zip(names, vals)}
        xp, h2p, g2p, kp, vp, sp, hp = _layer(xp, c_prompt, None, None, None, None, rel_bias, p)
        xs, h2s, g2s, kn, vn, sn, hn = _layer(xs, c_sample, cache_k_win[l], cache_v_win[l], state_wkv[l],
                                              state_shift[l], rel_bias, p)
        n_p = h2p.shape[0] * h2p.shape[1]
        h2 = jnp.concatenate([h2p.reshape(-1, D_MODEL), h2s.reshape(-1, D_MODEL)], axis=0)
        eid, gate_tile = _peer_route(h2, p['w_peer_q'], p['peer_sub_keys'])
        ffn = peer_experts(h2, eid, gate_tile, p['expert_u'], p['expert_v'])
        xp = _resid(xp, g2p, ffn[:n_p].reshape(xp.shape))
        xs = _resid(xs, g2s, ffn[n_p:].reshape(xs.shape))
        kp_l.append(kp); vp_l.append(vp); sp_l.append(sp); hp_l.append(hp)
        ks_l.append(kn); vs_l.append(vn); ss_l.append(sn); hs_l.append(hn)
    return (xp, xs, jnp.stack(kp_l), jnp.stack(vp_l), jnp.stack(sp_l), jnp.stack(hp_l),
            jnp.stack(ks_l), jnp.stack(vs_l), jnp.stack(ss_l), jnp.stack(hs_l))
```

```python
import functools
import math
import jax, jax.numpy as jnp
from jax import lax
import numpy as np
from jax.experimental import pallas as pl
from jax.experimental.pallas import tpu as pltpu

D_MODEL = 1024
DEPTH = 1
HEAD_DIM = 64
N_HEADS_ATTN = 8
N_HEADS_RWKV = 8
C_ATTN = N_HEADS_ATTN * HEAD_DIM
C_RWKV = N_HEADS_RWKV * HEAD_DIM
DILATIONS = ((128, 1), (512, 4), (2048, 16))
MAX_WINDOW = 2048
N_BUCKETS = 32
MAX_DISTANCE = 2048
LORA_DECAY = 32
LORA_ICLR = 32
LORA_GATE = 64
COLS_RWKV = 3 * C_RWKV + LORA_DECAY + LORA_ICLR + LORA_GATE
D_IN = 3 * C_ATTN + COLS_RWKV
PEER_HEADS = 8
PEER_KEYS = 128
PEER_QDIM = 256
PEER_HALF = PEER_QDIM // 2
PEER_TOPK = 16
PEER_CHUNK = 256
NORM_EPS = 1e-6
GN_EPS = 64e-5
NEG_INF = -1e30
ATTN_SCALE = HEAD_DIM ** -0.5


def _rms(x, g):
    x32 = x.astype(jnp.float32)
    return x32 * lax.rsqrt(jnp.mean(x32 * x32, axis=-1, keepdims=True) + NORM_EPS) * g


def _t5_bucket(dist):
    dist = np.asarray(dist, dtype=np.int64)
    max_exact = N_BUCKETS // 2
    safe = np.maximum(dist, 1) / max_exact
    large = max_exact + (np.log(safe) / math.log(MAX_DISTANCE / max_exact) * (N_BUCKETS - max_exact)).astype(np.int64)
    large = np.minimum(large, N_BUCKETS - 1)
    return np.where(dist < max_exact, dist, large).astype(np.int32)


LANES = 128
SUBLANES = 8


def _segment_mean_matrix(width, seg):
    lane = np.arange(width)
    return jnp.asarray((lane[:, None] // seg == lane[None, :] // seg).astype(np.float32) / seg, jnp.bfloat16)


def _head_rms(x, seg_ref, gain):
    x2 = x * x
    hi = x2.astype(jnp.bfloat16)
    lo = (x2 - hi.astype(jnp.float32)).astype(jnp.bfloat16)
    ms = (jnp.dot(hi, seg_ref[...], preferred_element_type=jnp.float32)
          + jnp.dot(lo, seg_ref[...], preferred_element_type=jnp.float32))
    return x * lax.rsqrt(ms + NORM_EPS) * gain


def _in_proj_kernel(x_ref, g_ref, sc_ref, sh_ref, w_ref, seg_ref, gq_ref, gk_ref, q_ref, k_ref, v_ref, r_ref):
    x = x_ref[0]
    h = x * lax.rsqrt(jnp.mean(x * x, axis=-1, keepdims=True) + NORM_EPS) * g_ref[...]
    h = (h * (1.0 + sc_ref[0]) + sh_ref[0]).astype(jnp.bfloat16)
    c = q_ref.shape[2]
    q_ref[0] = _head_rms(jnp.dot(h, w_ref[:, :c], preferred_element_type=jnp.float32), seg_ref, gq_ref[...])
    k_ref[0] = _head_rms(jnp.dot(h, w_ref[:, c:2 * c], preferred_element_type=jnp.float32), seg_ref, gk_ref[...])
    v_ref[0] = jnp.dot(h, w_ref[:, 2 * c:3 * c], preferred_element_type=jnp.float32)
    r_ref[0] = jnp.dot(h, w_ref[:, 3 * c:], preferred_element_type=jnp.float32)


def _in_proj(x, gain, sc, sh, w_in, q_gain, k_gain):
    B, T, D = x.shape
    n_out = w_in.shape[1]
    c = C_ATTN
    tm = min(T, 512)
    assert T % tm == 0
    heads = c // HEAD_DIM
    row = lambda a: a.reshape(1, -1)
    blk = lambda w: pl.BlockSpec((1, tm, w), lambda b, t: (b, t, 0))
    const = lambda a: pl.BlockSpec(a.shape, lambda b, t: (0,) * a.ndim)
    mod = pl.BlockSpec((1, 1, D), lambda b, t: (b, 0, 0))
    seg = _segment_mean_matrix(c, HEAD_DIM)
    wb = w_in.astype(jnp.bfloat16)
    gq, gk = row(jnp.tile(q_gain, heads)), row(jnp.tile(k_gain, heads))
    return pl.pallas_call(
        _in_proj_kernel,
        grid=(B, T // tm),
        in_specs=[blk(D), const(row(gain)), mod, mod, const(wb), const(seg), const(gq), const(gk)],
        out_specs=[blk(c), blk(c), blk(c), blk(n_out - 3 * c)],
        out_shape=[jax.ShapeDtypeStruct((B, T, c), jnp.float32)] * 3
                  + [jax.ShapeDtypeStruct((B, T, n_out - 3 * c), jnp.float32)],
        compiler_params=pltpu.CompilerParams(dimension_semantics=("parallel", "parallel"),
                                             vmem_limit_bytes=56 * 1024 * 1024),
        name="in_proj",
    )(x, row(gain), sc.reshape(B, 1, D), sh.reshape(B, 1, D), wb, seg, gq, gk)


def _sample_attn_kernel(q_ref, kn_ref, vn_ref, k_ref, v_ref, b1_ref, b2_ref, b3_ref, bn_ref, o_ref,
                        d_scr, dn_scr, p_scr, pn_scr):
    _, H, S, C = q_ref.shape
    P = k_ref.shape[3]
    for h in range(H):
        rows = pl.ds(h * S, S)
        d_scr[rows, :] = jnp.dot(q_ref[0, h], k_ref[0, h], preferred_element_type=jnp.float32) * ATTN_SCALE
        dn_scr[rows, :] = jnp.dot(q_ref[0, h], kn_ref[0, h], preferred_element_type=jnp.float32) * ATTN_SCALE
    d, dn = d_scr[...], dn_scr[...]
    ecs, ens, dens, lses = [], [], [], []
    for i, b_ref in enumerate((b1_ref, b2_ref, b3_ref)):
        w = b_ref.shape[1]
        lc = d[:, P - w:] + b_ref[...]
        ln = dn + bn_ref[i]
        m = jnp.maximum(jnp.max(lc, axis=1, keepdims=True), jnp.max(ln, axis=1, keepdims=True))
        ecs.append(jnp.exp(lc - m))
        ens.append(jnp.exp(ln - m))
        dens.append(jnp.sum(ecs[i], axis=1, keepdims=True) + jnp.sum(ens[i], axis=1, keepdims=True))
        lses.append(m + jnp.log(dens[i]))
    mm = jnp.maximum(jnp.maximum(lses[0], lses[1]), lses[2])
    ws = [jnp.exp(l - mm) for l in lses]
    wsum = ws[0] + ws[1] + ws[2]
    coef = [ws[i] / (wsum * dens[i]) for i in range(3)]
    w1, w2 = b1_ref.shape[1], b2_ref.shape[1]
    p3 = ecs[2] * coef[2]
    p2 = ecs[1] * coef[1]
    p_scr[:, :P - w2] = p3[:, :P - w2]
    p_scr[:, P - w2:P - w1] = p3[:, P - w2:P - w1] + p2[:, :w2 - w1]
    p_scr[:, P - w1:] = p3[:, P - w1:] + p2[:, w2 - w1:] + ecs[0] * coef[0]
    pn_scr[...] = ens[0] * coef[0] + ens[1] * coef[1] + ens[2] * coef[2]
    lane = lax.broadcasted_iota(jnp.int32, (C, S), 1)
    for h in range(H):
        o_tile = jnp.zeros((C, S), jnp.float32)
        for s in range(S):
            r = h * S + s
            acc = None
            for j in range(P // LANES):
                t = v_ref[0, h, :, j * LANES:(j + 1) * LANES] * p_scr[r:r + 1, j * LANES:(j + 1) * LANES]
                acc = t if acc is None else acc + t
            col = jnp.sum(acc, axis=1, keepdims=True) + jnp.sum(vn_ref[0, h] * pn_scr[r:r + 1, :], axis=1, keepdims=True)
            o_tile = jnp.where(lane == s, col, o_tile)
        o_ref[0, h] = o_tile


def _bias_by_distance(rel_bias, dist, valid):
    onehot = np.eye(N_BUCKETS, dtype=np.float32)[_t5_bucket(np.maximum(dist, 0))]
    b = jnp.dot(onehot, rel_bias.astype(jnp.float32), precision=lax.Precision.HIGHEST)
    return jnp.where(valid[:, None], b, NEG_INF)


def _sample_bias_tables(rel_bias, S, lb):
    tabs, news = [], []
    H = rel_bias.shape[1]
    for window, dil in DILATIONS:
        d = np.arange(window + S + 1)
        vec = _bias_by_distance(rel_bias, d, (d % dil == 0) & (d >= 1) & (d <= window))
        rows = [vec[s + 1:s + 1 + window][::-1] for s in range(S)]
        tabs.append(jnp.transpose(jnp.stack(rows), (2, 0, 1)).reshape(H * S, window))
        dn = (np.arange(S)[:, None] - np.arange(S)[None, :]).reshape(-1)
        bn = _bias_by_distance(rel_bias, dn, (dn >= 0) & (dn % dil == 0) & (dn <= window)).reshape(S, S, H)
        news.append(jnp.transpose(bn, (2, 0, 1)).reshape(H * S, S))
    return tabs, jnp.stack(news)


def _sample_attn(q, k_new, v_new, k_buf_t, v_buf_t, rel_bias):
    B, S, H, C = q.shape
    lb = k_buf_t.shape[3]
    assert lb >= DILATIONS[-1][0] and all(w % LANES == 0 for w, _ in DILATIONS)
    tabs, bn = _sample_bias_tables(rel_bias, S, lb)
    tr = lambda a: jnp.transpose(a, (0, 2, 3, 1))
    qs = pl.BlockSpec((1, H, S, C), lambda b: (b, 0, 0, 0))
    tok = pl.BlockSpec((1, H, C, S), lambda b: (b, 0, 0, 0))
    cache = pl.BlockSpec((1, H, C, lb), lambda b: (b, 0, 0, 0))
    full = lambda a: pl.BlockSpec(a.shape, lambda b: (0,) * a.ndim)
    o = pl.pallas_call(
        _sample_attn_kernel,
        grid=(B,),
        in_specs=[qs, tok, tok, cache, cache, full(tabs[0]), full(tabs[1]), full(tabs[2]), full(bn)],
        out_specs=tok,
        out_shape=jax.ShapeDtypeStruct((B, H, C, S), jnp.float32),
        scratch_shapes=[pltpu.VMEM((H * S, lb), jnp.float32), pltpu.VMEM((H * S, S), jnp.float32),
                        pltpu.VMEM((H * S, lb), jnp.float32), pltpu.VMEM((H * S, S), jnp.float32)],
        compiler_params=pltpu.CompilerParams(dimension_semantics=("parallel",),
                                             vmem_limit_bytes=48 * 1024 * 1024),
        name="sample_attn",
    )(jnp.transpose(q, (0, 2, 1, 3)), tr(k_new), tr(v_new), k_buf_t, v_buf_t, *tabs, bn)
    return jnp.transpose(o, (0, 3, 1, 2))


def _prompt_attn_kernel(q_ref, kp_ref, kc_ref, vp_ref, vc_ref, bias_ref, o_ref, lse_ref):
    g = pl.program_id(2)
    n = q_ref.shape[1]
    heads = bias_ref.shape[0]
    lane = lax.broadcasted_iota(jnp.int32, (n, LANES), 1)
    nt = (((1,), (1,)), ((), ()))
    per = LANES // HEAD_DIM
    sls = [slice(hp * LANES, (hp + 1) * LANES) for hp in range(heads // per)]
    keeps = [(lane >= half * HEAD_DIM) & (lane < (half + 1) * HEAD_DIM) for half in range(per)]
    logits = []
    for h in range(heads):
        sl, keep = sls[h // per], keeps[h % per]
        qh = jnp.where(keep, q_ref[0, :, sl], 0.0)
        lp = lax.dot_general(qh, kp_ref[0, :, sl], nt, preferred_element_type=jnp.float32) * ATTN_SCALE + bias_ref[h, :, :n]
        lc = lax.dot_general(qh, kc_ref[0, :, sl], nt, preferred_element_type=jnp.float32) * ATTN_SCALE + bias_ref[h, :, n:]
        logits.append((jnp.where(g == 0, NEG_INF, lp), lc))
    probs = []
    for lp, lc in logits:
        m = jnp.max(jnp.maximum(lp, lc), axis=-1, keepdims=True)
        ep = jnp.exp(lp - m)
        ec = jnp.exp(lc - m)
        s = jnp.sum(ep + ec, axis=-1, keepdims=True)
        probs.append((ep, ec, s, m + jnp.log(s)))
    for hp, sl in enumerate(sls):
        o2 = jnp.zeros((n, LANES), jnp.float32)
        l2 = jnp.zeros((n, LANES), jnp.float32)
        for half in range(per):
            ep, ec, s, lse = probs[hp * per + half]
            o = (jnp.dot(ep, vp_ref[0, :, sl], preferred_element_type=jnp.float32)
                 + jnp.dot(ec, vc_ref[0, :, sl], preferred_element_type=jnp.float32)) / s
            o2 = jnp.where(keeps[half], o, o2)
            l2 = jnp.where(keeps[half], lse, l2)
        o_ref[0, :, sl] = o2
        lse_ref[0, :, sl] = l2


def _merge_kernel(o1, o2, o3, l1, l2, l3, out):
    m = jnp.maximum(jnp.maximum(l1[...], l2[...]), l3[...])
    w1, w2, w3 = jnp.exp(l1[...] - m), jnp.exp(l2[...] - m), jnp.exp(l3[...] - m)
    ws = w1 + w2 + w3
    out[...] = (w1 / ws) * o1[...] + (w2 / ws) * o2[...] + (w3 / ws) * o3[...]


def _prompt_bias(rel_bias, n, dil):
    H = rel_bias.shape[1]
    m = 3 * n
    j = 2 * n - 1 - np.arange(m)
    u = _bias_by_distance(rel_bias, j * dil, (j >= 0) & (j <= n))
    rows = jnp.tile(u, (n, 1))[:n * (m - 1)].reshape(n, m - 1, H)
    return jnp.transpose(rows[:, n - 1:3 * n - 1], (2, 0, 1))


def _prompt_attn(q, k, v, rel_bias):
    B, T, HC = q.shape
    H = HC // HEAD_DIM
    outs, lses = [], []
    for window, dil in DILATIONS:
        n = window // dil
        assert T % window == 0 and HC % LANES == 0
        G = T // window
        view = lambda a: a.reshape(B, T // dil, dil * HC)
        cur = pl.BlockSpec((1, n, HC), lambda b, r, g: (b, g, r))
        prev = pl.BlockSpec((1, n, HC), lambda b, r, g: (b, jnp.maximum(g - 1, 0), r))
        bias = _prompt_bias(rel_bias, n, dil)
        o, lse = pl.pallas_call(
            _prompt_attn_kernel,
            grid=(B, dil, G),
            in_specs=[cur, prev, cur, prev, cur, pl.BlockSpec((H, n, 2 * n), lambda b, r, g: (0, 0, 0))],
            out_specs=[cur, cur],
            out_shape=[jax.ShapeDtypeStruct((B, T // dil, dil * HC), jnp.float32)] * 2,
            compiler_params=pltpu.CompilerParams(dimension_semantics=("parallel", "parallel", "arbitrary")),
            name="prompt_attn_d%d" % dil,
        )(view(q), view(k), view(k), view(v), view(v), bias)
        outs.append(o.reshape(B, T, HC))
        lses.append(lse.reshape(B, T, HC))
    tb = 512
    blk = pl.BlockSpec((1, tb, HC), lambda b, t: (b, t, 0))
    return pl.pallas_call(
        _merge_kernel, grid=(B, T // tb), in_specs=[blk] * 6, out_specs=blk,
        out_shape=jax.ShapeDtypeStruct((B, T, HC), jnp.float32),
        compiler_params=pltpu.CompilerParams(dimension_semantics=("parallel", "parallel")),
        name="prompt_attn_merge",
    )(*outs, *lses)


WKV_ROWS_PER_PASS = 4


def _wkv_kernel(w_ref, nkk_ref, b_ref, k_ref, r_ref, vop_ref, s0_ref, y_ref, s_ref, *, tt, vp, kd):
    j = pl.program_id(1)
    kgs = kd // SUBLANES
    G = WKV_ROWS_PER_PASS

    @pl.when(j == 0)
    def _():
        s_ref[...] = s0_ref[...]

    def step(t, c):
        for v0 in range(0, vp, G):
            accs = [None] * G
            for kg in range(kgs):
                ks = pl.ds(kg * SUBLANES, SUBLANES)
                nkk = nkk_ref[t, ks, :]
                for i in range(G):
                    p = s_ref[v0 + i, ks, :] * nkk
                    accs[i] = p if accs[i] is None else accs[i] + p
            sa = [jnp.sum(a, axis=0, keepdims=True) for a in accs]
            vv = [vop_ref[t, pl.ds(v0 + i, 1), :] for i in range(G)]
            yacc = [None] * G
            for kg in range(kgs):
                ks = pl.ds(kg * SUBLANES, SUBLANES)
                w = w_ref[t, ks, :]
                b = b_ref[t, ks, :]
                k = k_ref[t, ks, :]
                r = r_ref[t, ks, :]
                for i in range(G):
                    s2 = s_ref[v0 + i, ks, :] * w + sa[i] * b + vv[i] * k
                    s_ref[v0 + i, ks, :] = s2
                    p = s2 * r
                    yacc[i] = p if yacc[i] is None else yacc[i] + p
            for i in range(G):
                y_ref[t, pl.ds(v0 + i, 1), :] = jnp.sum(yacc[i], axis=0, keepdims=True)
        return c

    lax.fori_loop(0, tt, step, 0)


def _wkv_scan(r, w, k, v, kk, a, s0):
    B, T, H, N = r.shape
    bh = B * H
    nvh = max(1, LANES // bh)
    L = nvh * bh
    assert L % LANES == 0 and N % (nvh * WKV_ROWS_PER_PASS) == 0 and N % SUBLANES == 0
    vp = N // nvh
    tt = T if T <= 32 else 32
    assert T % tt == 0

    def kform(x):
        x = jnp.transpose(x, (1, 3, 0, 2)).reshape(T, N, 1, bh)
        return jnp.broadcast_to(x, (T, N, nvh, bh)).reshape(T, N, L)

    kops = [kform(w), kform(-kk), kform(kk * a), kform(k), kform(r)]
    vop = jnp.transpose(v, (1, 3, 0, 2)).reshape(T, nvh, vp, bh)
    vop = jnp.transpose(vop, (0, 2, 1, 3)).reshape(T, vp, L)
    s0t = jnp.transpose(s0, (2, 3, 0, 1)).reshape(nvh, vp, N, bh)
    s0t = jnp.transpose(s0t, (1, 2, 0, 3)).reshape(vp, N, L)
    kern = functools.partial(_wkv_kernel, tt=tt, vp=vp, kd=N)
    y, sf = pl.pallas_call(
        kern,
        grid=(L // LANES, T // tt),
        in_specs=[pl.BlockSpec((tt, N, LANES), lambda l, j: (j, 0, l))] * 5 + [
                  pl.BlockSpec((tt, vp, LANES), lambda l, j: (j, 0, l)),
                  pl.BlockSpec((vp, N, LANES), lambda l, j: (0, 0, l))],
        out_specs=[pl.BlockSpec((tt, vp, LANES), lambda l, j: (j, 0, l)),
                   pl.BlockSpec((vp, N, LANES), lambda l, j: (0, 0, l))],
        out_shape=[jax.ShapeDtypeStruct((T, vp, L), jnp.float32),
                   jax.ShapeDtypeStruct((vp, N, L), jnp.float32)],
        compiler_params=pltpu.CompilerParams(dimension_semantics=("parallel", "arbitrary"),
                                             vmem_limit_bytes=48 * 1024 * 1024),
        name="wkv_scan",
    )(*kops, vop, s0t)
    y = jnp.transpose(y.reshape(T, vp, nvh, B, H), (3, 0, 4, 2, 1)).reshape(B, T, H, N)
    sf = jnp.transpose(sf.reshape(vp, N, nvh, B, H), (3, 4, 2, 0, 1)).reshape(B, H, N, N)
    return y, sf


def _rwkv_mixer(pb, shift0, s0, mu, w0, w_w2, a0, w_a2, w_g2, k_k, k_a, r_k, lnx_g, lnx_b):
    B, T, _ = pb.shape
    pb = pb.astype(jnp.float32)
    prev = jnp.concatenate([shift0[:, None, :].astype(jnp.float32), pb[:, :-1]], axis=1)
    xm = pb + (prev - pb) * mu
    c = C_RWKV
    r, k, v = xm[..., :c], xm[..., c:2 * c], xm[..., 2 * c:3 * c]
    o = 3 * c
    wl = xm[..., o:o + LORA_DECAY]
    al = xm[..., o + LORA_DECAY:o + LORA_DECAY + LORA_ICLR]
    gl = xm[..., o + LORA_DECAY + LORA_ICLR:]
    w_log = -jax.nn.softplus(-(w0 + jnp.tanh(wl) @ w_w2)) - 0.5
    decay = jnp.exp(-jnp.exp(w_log))
    a = jax.nn.sigmoid(a0 + al @ w_a2)
    g = jax.nn.sigmoid(gl) @ w_g2

    def heads(t):
        return t.reshape(B, T, N_HEADS_RWKV, HEAD_DIM)
    kk = heads(k * k_k)
    kk = kk / jnp.maximum(jnp.sqrt(jnp.sum(kk * kk, axis=-1, keepdims=True)), 1e-12)
    k = k * (1.0 + (a - 1.0) * k_a)
    r_h, k_h, v_h = heads(r), heads(k), heads(v)
    y, s_fin = _wkv_scan(r_h, heads(decay), k_h, v_h, kk, heads(a), s0.astype(jnp.float32))
    mean = jnp.mean(y, axis=-1, keepdims=True)
    var = jnp.mean(jnp.square(y - mean), axis=-1, keepdims=True)
    y = ((y - mean) * lax.rsqrt(var + GN_EPS)).reshape(B, T, C_RWKV) * lnx_g + lnx_b
    bonus = jnp.sum(r_h * k_h * r_k, axis=-1, keepdims=True) * v_h
    y = (y + bonus.reshape(B, T, C_RWKV)) * g
    return y, s_fin, pb[:, -1]


def _gelu_exact(x):
    return 0.5 * x * (1.0 + lax.erf(x * (2.0 ** -0.5)))


PEER_TOK = 8


def _peer_expert_kernel(eidc_ref, eidn_ref, h_ref, gate_ref, tab_ref, out_ref, buf, sem, *, rows, dim):
    i = pl.program_id(0)
    n = pl.num_programs(0)
    tok = PEER_TOK
    chunks = dim // LANES
    erow = chunks
    groups = rows // SUBLANES
    slot_groups = tok * groups

    def row_copy(eid_ref, src_t, t, r, slot):
        e = eid_ref[src_t, r]
        src = tab_ref.at[pl.ds(pl.multiple_of(e * erow, erow), erow)]
        dst = buf.at[slot * slot_groups + t * groups + r // SUBLANES, :, r % SUBLANES, :]
        return pltpu.make_async_copy(src, dst, sem.at[slot])

    def wait_slot(slot):
        region = buf.at[pl.ds(slot * slot_groups, slot_groups)]
        pltpu.make_async_copy(region, region, sem.at[slot]).wait()

    def compute(t_blk, t, slot):
        g0 = slot * slot_groups + t * groups
        hb = [jnp.broadcast_to(h_ref[t_blk:t_blk + 1, k * LANES:(k + 1) * LANES], (SUBLANES, LANES))
              for k in range(chunks)]
        lane = lax.broadcasted_iota(jnp.int32, (SUBLANES, LANES), 1)
        hi_mask = jnp.uint32(0xFFFF0000)
        s_tile = jnp.zeros((SUBLANES, LANES), jnp.float32)
        for g in range(groups):
            acc = None
            for k in range(chunks):
                p = pltpu.bitcast(buf[g0 + g, k] & hi_mask, jnp.float32) * hb[k]
                acc = p if acc is None else acc + p
            s_tile = jnp.where(lane == g, jnp.sum(acc, axis=1, keepdims=True), s_tile)
        w_tile = _gelu_exact(s_tile) * gate_ref[t_blk]
        accs = [None] * chunks
        for g in range(groups):
            wg = jnp.broadcast_to(w_tile[:, g:g + 1], (SUBLANES, LANES))
            for k in range(chunks):
                p = wg * pltpu.bitcast(buf[g0 + g, k] << 16, jnp.float32)
                accs[k] = p if accs[k] is None else accs[k] + p
        out_ref[t_blk:t_blk + 1, :] = jnp.concatenate([jnp.sum(a, axis=0, keepdims=True) for a in accs], axis=1)

    @pl.when(i == 0)
    def _():
        for t in range(tok):
            for r in range(rows):
                row_copy(eidc_ref, t, t, r, 0).start()

    wait_slot(0)
    for t in range(tok):
        for r in range(rows):
            row_copy(eidc_ref, tok + t, t, r, 1).start()
        compute(t, t, 0)
    wait_slot(1)
    for t in range(tok):
        for r in range(rows):
            row_copy(eidn_ref, t, t, r, 0).start()
        compute(tok + t, t, 1)

    @pl.when(i == n - 1)
    def _():
        wait_slot(0)


def peer_experts(h, eid, gate_tile, expert_u, expert_v):
    N, D = h.shape
    R = eid.shape[1]
    E = expert_u.shape[0]
    chunks = D // LANES
    step_tok = 2 * PEER_TOK
    assert N % step_tok == 0 and R % SUBLANES == 0 and D % LANES == 0
    groups = R // SUBLANES

    def bf16_bits(x):
        return lax.bitcast_convert_type(x.astype(jnp.bfloat16), jnp.uint16).astype(jnp.uint32)
    tab = ((bf16_bits(expert_u) << 16) | bf16_bits(expert_v)).reshape(E * chunks, LANES)
    nsteps = N // step_tok
    kern = functools.partial(_peer_expert_kernel, rows=R, dim=D)
    return pl.pallas_call(
        kern,
        grid=(nsteps,),
        in_specs=[
            pl.BlockSpec((step_tok, R), lambda i: (i, 0), memory_space=pltpu.SMEM),
            pl.BlockSpec((step_tok, R), lambda i: (jnp.minimum(i + 1, nsteps - 1), 0), memory_space=pltpu.SMEM),
            pl.BlockSpec((step_tok, D), lambda i: (i, 0)),
            pl.BlockSpec((step_tok, SUBLANES, LANES), lambda i: (i, 0, 0)),
            pl.BlockSpec(memory_space=pl.ANY),
        ],
        out_specs=pl.BlockSpec((step_tok, D), lambda i: (i, 0)),
        out_shape=jax.ShapeDtypeStruct((N, D), jnp.float32),
        scratch_shapes=[pltpu.VMEM((2 * PEER_TOK * groups, chunks, SUBLANES, LANES), jnp.uint32),
                        pltpu.SemaphoreType.DMA((2,))],
        compiler_params=pltpu.CompilerParams(dimension_semantics=("arbitrary",),
                                             vmem_limit_bytes=48 * 1024 * 1024,
                                             disable_bounds_checks=True),
        name="peer_experts",
    )(eid, eid, h, gate_tile, tab)


PEER_PAIRS = tuple((a, b) for a in range(PEER_TOPK) for b in range(PEER_TOPK) if (a + 1) * (b + 1) <= PEER_TOPK)


def _top_rows(x, pos, k):
    vals, idxs = [], []
    for _ in range(k):
        m = jnp.max(x, axis=0, keepdims=True)
        i = jnp.min(jnp.where(x == m, pos, jnp.float32(1e9)), axis=0, keepdims=True)
        vals.append(m)
        idxs.append(i)
        x = jnp.where(pos == i, -jnp.inf, x)
    return vals, idxs


def _joint_topk(s_ref, eid_ref, gate_ref, *, heads, keys, topk):
    tb = s_ref.shape[1]
    key_pos = lax.broadcasted_iota(jnp.int32, (keys, tb), 0).astype(jnp.float32)
    nrow = -(-len(PEER_PAIRS) // SUBLANES) * SUBLANES
    pair_pos = lax.broadcasted_iota(jnp.int32, (nrow, tb), 0).astype(jnp.float32)
    for h in range(heads):
        v1, i1 = _top_rows(s_ref[pl.ds((2 * h) * keys, keys), :], key_pos, topk)
        v2, i2 = _top_rows(s_ref[pl.ds((2 * h + 1) * keys, keys), :], key_pos, topk)
        cand = jnp.full((nrow, tb), -jnp.inf, jnp.float32)
        cid = jnp.zeros((nrow, tb), jnp.float32)
        for p, (a, b) in enumerate(PEER_PAIRS):
            cand = jnp.where(pair_pos == p, v1[a] + v2[b], cand)
            cid = jnp.where(pair_pos == p, i1[a] * keys + i2[b], cid)
        top_s, top_p = _top_rows(cand, pair_pos, topk)
        es = [jnp.exp(s - top_s[0]) for s in top_s]
        den = es[0]
        for e in es[1:]:
            den = den + e
        for r in range(topk):
            eid = jnp.sum(jnp.where(pair_pos == top_p[r], cid, 0.0), axis=0, keepdims=True)
            eid_ref[pl.ds(h * topk + r, 1), :] = eid.astype(jnp.int32)
            gate_ref[pl.ds(h * topk + r, 1), :] = es[r] / den


def _route_kernel(h_ref, wq_ref, sk_ref, eid_ref, gate_ref, s_scr, *, heads, keys, topk):
    q = jnp.dot(h_ref[...].astype(jnp.bfloat16), wq_ref[...], preferred_element_type=jnp.float32)
    half = sk_ref.shape[2]
    nt = (((1,), (1,)), ((), ()))
    for g in range(2 * heads):
        qg = q[:, g * half:(g + 1) * half].astype(jnp.bfloat16)
        s_scr[pl.ds(g * keys, keys), :] = lax.dot_general(sk_ref[g], qg, nt, preferred_element_type=jnp.float32)
    _joint_topk(s_scr, eid_ref, gate_ref, heads=heads, keys=keys, topk=topk)


def _peer_route(h, w_pq, sub_keys, *, tb=LANES):
    n_tok, D = h.shape
    heads, _, keys, half = sub_keys.shape
    assert n_tok % tb == 0
    wq = w_pq.astype(jnp.bfloat16)
    sk = sub_keys.reshape(heads * 2, keys, half).astype(jnp.bfloat16)
    kern = functools.partial(_route_kernel, heads=heads, keys=keys, topk=PEER_TOPK)
    const = lambda a: pl.BlockSpec(a.shape, lambda i: (0,) * a.ndim)
    eid_t, gate_t = pl.pallas_call(
        kern,
        grid=(n_tok // tb,),
        in_specs=[pl.BlockSpec((tb, D), lambda i: (i, 0)), const(wq), const(sk)],
        out_specs=[pl.BlockSpec((heads * PEER_TOPK, tb), lambda i: (0, i)),
                   pl.BlockSpec((heads * PEER_TOPK, tb), lambda i: (0, i))],
        out_shape=[jax.ShapeDtypeStruct((heads * PEER_TOPK, n_tok), jnp.int32),
                   jax.ShapeDtypeStruct((heads * PEER_TOPK, n_tok), jnp.float32)],
        scratch_shapes=[pltpu.VMEM((heads * 2 * keys, tb), jnp.float32)],
        compiler_params=pltpu.CompilerParams(dimension_semantics=("parallel",),
                                             vmem_limit_bytes=48 * 1024 * 1024),
        name="peer_route",
    )(h, wq, sk)
    groups = PEER_HEADS * PEER_TOPK // SUBLANES
    gate_tile = jnp.transpose(gate_t.reshape(groups, SUBLANES, n_tok), (2, 1, 0))
    gate_tile = jnp.pad(gate_tile, ((0, 0), (0, 0), (0, LANES - groups)))
    return eid_t.T, gate_tile


def _mix_proj_kernel(oa_ref, yr_ref, x_ref, g1_ref, sc_ref, sh_ref, gn_ref, wa_ref, wr_ref, x1_ref, h2_ref):
    mix = (jnp.dot(oa_ref[0].astype(jnp.bfloat16), wa_ref[...], preferred_element_type=jnp.float32)
           + jnp.dot(yr_ref[0].astype(jnp.bfloat16), wr_ref[...], preferred_element_type=jnp.float32))
    x1 = x_ref[0] + g1_ref[0] * mix
    x1_ref[0] = x1
    h = x1 * lax.rsqrt(jnp.mean(x1 * x1, axis=-1, keepdims=True) + NORM_EPS) * gn_ref[...]
    h2_ref[0] = h * (1.0 + sc_ref[0]) + sh_ref[0]


def _mix_proj(o_attn, y_rwkv, x, g1, sc2, sh2, gain2, w_out):
    B, T, D = x.shape
    ca, cr = o_attn.shape[2], y_rwkv.shape[2]
    tm = min(T, 512)
    assert T % tm == 0
    blk = lambda w: pl.BlockSpec((1, tm, w), lambda b, t: (b, t, 0))
    mod = pl.BlockSpec((1, 1, D), lambda b, t: (b, 0, 0))
    const = lambda a: pl.BlockSpec(a.shape, lambda b, t: (0,) * a.ndim)
    wa, wr = w_out[:ca].astype(jnp.bfloat16), w_out[ca:].astype(jnp.bfloat16)
    gn = gain2.reshape(1, D)
    return pl.pallas_call(
        _mix_proj_kernel,
        grid=(B, T // tm),
        in_specs=[blk(ca), blk(cr), blk(D), mod, mod, mod, const(gn), const(wa), const(wr)],
        out_specs=[blk(D), blk(D)],
        out_shape=[jax.ShapeDtypeStruct((B, T, D), jnp.float32)] * 2,
        compiler_params=pltpu.CompilerParams(dimension_semantics=("parallel", "parallel"),
                                             vmem_limit_bytes=48 * 1024 * 1024),
        name="mix_proj",
    )(o_attn, y_rwkv, x, g1.reshape(B, 1, D), sc2.reshape(B, 1, D), sh2.reshape(B, 1, D), gn, wa, wr)


def _resid_kernel(x_ref, g_ref, f_ref, o_ref):
    o_ref[...] = x_ref[...] + g_ref[...] * f_ref[...]


def _resid(x, g, f):
    B, T, D = x.shape
    tb = min(T, 512)
    return pl.pallas_call(
        _resid_kernel,
        grid=(B, T // tb),
        in_specs=[pl.BlockSpec((1, tb, D), lambda b, t: (b, t, 0)),
                  pl.BlockSpec((1, 1, D), lambda b, t: (b, 0, 0)),
                  pl.BlockSpec((1, tb, D), lambda b, t: (b, t, 0))],
        out_specs=pl.BlockSpec((1, tb, D), lambda b, t: (b, t, 0)),
        out_shape=jax.ShapeDtypeStruct(x.shape, x.dtype),
        name="ffn_residual",
    )(x, g, f)


def _layer(x, c, k_buf, v_buf, wkv0, shift0, rel_bias, p):
    B, T, _ = x.shape
    mod = jax.nn.silu(c.astype(jnp.float32)) @ p['ada_w'] + p['ada_b']
    sh1, sc1, g1, sh2, sc2, g2 = jnp.split(mod[:, None, :], 6, axis=-1)
    q3, k3, v3, proj_rwkv = _in_proj(x, p['norm1_g'], sc1, sh1, p['w_in'], p['q_norm_g'], p['k_norm_g'])
    q, k, v = (a.reshape(B, T, N_HEADS_ATTN, HEAD_DIM) for a in (q3, k3, v3))
    if k_buf is None:
        o_attn = _prompt_attn(q3, k3, v3, rel_bias)
        keep = min(MAX_WINDOW, T)
        k_rows, v_rows = k[:, T - keep:], v[:, T - keep:]
        wkv0 = jnp.zeros((B, N_HEADS_RWKV, HEAD_DIM, HEAD_DIM), jnp.float32)
        shift0 = jnp.zeros((B, COLS_RWKV), jnp.float32)
    else:
        o_attn = _sample_attn(q, k, v, jnp.transpose(k_buf, (0, 2, 3, 1)), jnp.transpose(v_buf, (0, 2, 3, 1)), rel_bias)
        k_rows, v_rows = k, v
    y_rwkv, s_fin, shift_new = _rwkv_mixer(
        proj_rwkv, shift0, wkv0, p['mu_shift'], p['w0'], p['w_w2'], p['a0'], p['w_a2'],
        p['w_g2'], p['k_k'], p['k_a'], p['r_k'], p['lnx_g'], p['lnx_b'])
    x, h2 = _mix_proj(o_attn.reshape(B, T, C_ATTN), y_rwkv, x, g1, sc2, sh2, p['norm2_g'], p['w_out'])
    return x, h2, g2, k_rows, v_rows, s_fin, shift_new


def kernel(x_prompt, x_sample, c_prompt, c_sample, cache_k_win, cache_v_win, state_wkv, state_shift,
           ada_w, ada_b, norm1_g, norm2_g, w_in, q_norm_g, k_norm_g, rel_bias, mu_shift, w0, w_w2, a0,
           w_a2, w_g2, k_k, k_a, r_k, lnx_g, lnx_b, w_out, w_peer_q, peer_sub_keys, expert_u, expert_v):
    xp, xs = x_prompt, x_sample
    kp_l, vp_l, sp_l, hp_l = [], [], [], []
    ks_l, vs_l, ss_l, hs_l = [], [], [], []
    names = ('ada_w', 'ada_b', 'norm1_g', 'norm2_g', 'w_in', 'q_norm_g', 'k_norm_g', 'mu_shift', 'w0', 'w_w2',
             'a0', 'w_a2', 'w_g2', 'k_k', 'k_a', 'r_k', 'lnx_g', 'lnx_b', 'w_out', 'w_peer_q', 'peer_sub_keys',
             'expert_u', 'expert_v')
    vals = (ada_w, ada_b, norm1_g, norm2_g, w_in, q_norm_g, k_norm_g, mu_shift, w0, w_w2, a0, w_a2, w_g2, k_k,
            k_a, r_k, lnx_g, lnx_b, w_out, w_peer_q, peer_sub_keys, expert_u, expert_v)
    for l in range(DEPTH):
        p = {n: v[l] for n, v in zip(names, vals)}
        xp, h2p, g2p, kp, vp, sp, hp = _layer(xp, c_prompt, None, None, None, None, rel_bias, p)
        xs, h2s, g2s, kn, vn, sn, hn = _layer(xs, c_sample, cache_k_win[l], cache_v_win[l], state_wkv[l],
                                              state_shift[l], rel_bias, p)
        n_p = h2p.shape[0] * h2p.shape[1]
        h2 = jnp.concatenate([h2p.reshape(-1, D_MODEL), h2s.reshape(-1, D_MODEL)], axis=0)
        eid, gate_tile = _peer_route(h2, p['w_peer_q'], p['peer_sub_keys'])
        ffn = peer_experts(h2, eid, gate_tile, p['expert_u'], p['expert_v'])
        xp = _resid(xp, g2p, ffn[:n_p].reshape(xp.shape))
        xs = _resid(xs, g2s, ffn[n_p:].reshape(xs.shape))
        kp_l.append(kp); vp_l.append(vp); sp_l.append(sp); hp_l.append(hp)
        ks_l.append(kn); vs_l.append(vn); ss_l.append(sn); hs_l.append(hn)
    return (xp, xs, jnp.stack(kp_l), jnp.stack(vp_l), jnp.stack(sp_l), jnp.stack(hp_l),
            jnp.stack(ks_l), jnp.stack(vs_l), jnp.stack(ss_l), jnp.stack(hs_l))
```

```python
import functools
import math
import jax, jax.numpy as jnp
from jax import lax
import numpy as np
from jax.experimental import pallas as pl
from jax.experimental.pallas import tpu as pltpu

D_MODEL = 1024
DEPTH = 1
HEAD_DIM = 64
N_HEADS_ATTN = 8
N_HEADS_RWKV = 8
C_ATTN = N_HEADS_ATTN * HEAD_DIM
C_RWKV = N_HEADS_RWKV * HEAD_DIM
DILATIONS = ((128, 1), (512, 4), (2048, 16))
MAX_WINDOW = 2048
N_BUCKETS = 32
MAX_DISTANCE = 2048
LORA_DECAY = 32
LORA_ICLR = 32
LORA_GATE = 64
COLS_RWKV = 3 * C_RWKV + LORA_DECAY + LORA_ICLR + LORA_GATE
D_IN = 3 * C_ATTN + COLS_RWKV
PEER_HEADS = 8
PEER_KEYS = 128
PEER_QDIM = 256
PEER_HALF = PEER_QDIM // 2
PEER_TOPK = 16
PEER_CHUNK = 256
NORM_EPS = 1e-6
GN_EPS = 64e-5
NEG_INF = -1e30
ATTN_SCALE = HEAD_DIM ** -0.5


def _rms(x, g):
    x32 = x.astype(jnp.float32)
    return x32 * lax.rsqrt(jnp.mean(x32 * x32, axis=-1, keepdims=True) + NORM_EPS) * g


def _t5_bucket(dist):
    dist = np.asarray(dist, dtype=np.int64)
    max_exact = N_BUCKETS // 2
    safe = np.maximum(dist, 1) / max_exact
    large = max_exact + (np.log(safe) / math.log(MAX_DISTANCE / max_exact) * (N_BUCKETS - max_exact)).astype(np.int64)
    large = np.minimum(large, N_BUCKETS - 1)
    return np.where(dist < max_exact, dist, large).astype(np.int32)


LANES = 128
SUBLANES = 8


def _mod_spec(a, tm):
    per_token = a.shape[1] != 1
    return pl.BlockSpec((1, tm if per_token else 1, a.shape[2]), lambda b, t: (b, t if per_token else 0, 0))


def _segment_mean_matrix(width, seg):
    lane = np.arange(width)
    return jnp.asarray((lane[:, None] // seg == lane[None, :] // seg).astype(np.float32) / seg, jnp.bfloat16)


def _head_rms(x, seg_ref, gain):
    x2 = x * x
    hi = x2.astype(jnp.bfloat16)
    lo = (x2 - hi.astype(jnp.float32)).astype(jnp.bfloat16)
    ms = (jnp.dot(hi, seg_ref[...], preferred_element_type=jnp.float32)
          + jnp.dot(lo, seg_ref[...], preferred_element_type=jnp.float32))
    return x * lax.rsqrt(ms + NORM_EPS) * gain


def _in_proj_kernel(x_ref, g_ref, sc_ref, sh_ref, w_ref, seg_ref, gq_ref, gk_ref, q_ref, k_ref, v_ref, r_ref):
    x = x_ref[0]
    h = x * lax.rsqrt(jnp.mean(x * x, axis=-1, keepdims=True) + NORM_EPS) * g_ref[...]
    h = (h * (1.0 + sc_ref[0]) + sh_ref[0]).astype(jnp.bfloat16)
    c = q_ref.shape[2]
    q_ref[0] = _head_rms(jnp.dot(h, w_ref[:, :c], preferred_element_type=jnp.float32), seg_ref, gq_ref[...])
    k_ref[0] = _head_rms(jnp.dot(h, w_ref[:, c:2 * c], preferred_element_type=jnp.float32), seg_ref, gk_ref[...])
    v_ref[0] = jnp.dot(h, w_ref[:, 2 * c:3 * c], preferred_element_type=jnp.float32)
    r_ref[0] = jnp.dot(h, w_ref[:, 3 * c:], preferred_element_type=jnp.float32)


def _in_proj(x, gain, sc, sh, w_in, q_gain, k_gain):
    B, T, D = x.shape
    n_out = w_in.shape[1]
    c = C_ATTN
    tm = min(T, 512)
    assert T % tm == 0
    heads = c // HEAD_DIM
    row = lambda a: a.reshape(1, -1)
    blk = lambda w: pl.BlockSpec((1, tm, w), lambda b, t: (b, t, 0))
    const = lambda a: pl.BlockSpec(a.shape, lambda b, t: (0,) * a.ndim)
    mod = _mod_spec(sc, tm)
    seg = _segment_mean_matrix(c, HEAD_DIM)
    wb = w_in.astype(jnp.bfloat16)
    gq, gk = row(jnp.tile(q_gain, heads)), row(jnp.tile(k_gain, heads))
    return pl.pallas_call(
        _in_proj_kernel,
        grid=(B, T // tm),
        in_specs=[blk(D), const(row(gain)), mod, mod, const(wb), const(seg), const(gq), const(gk)],
        out_specs=[blk(c), blk(c), blk(c), blk(n_out - 3 * c)],
        out_shape=[jax.ShapeDtypeStruct((B, T, c), jnp.float32)] * 3
                  + [jax.ShapeDtypeStruct((B, T, n_out - 3 * c), jnp.float32)],
        compiler_params=pltpu.CompilerParams(dimension_semantics=("parallel", "parallel"),
                                             vmem_limit_bytes=56 * 1024 * 1024),
        name="in_proj",
    )(x, row(gain), sc, sh, wb, seg, gq, gk)


def _sample_attn_kernel(q_ref, kn_ref, vn_ref, k_ref, v_ref, b1_ref, b2_ref, b3_ref, bn_ref, o_ref,
                        d_scr, dn_scr, p_scr, pn_scr):
    _, H, S, C = q_ref.shape
    P = k_ref.shape[3]
    for h in range(H):
        rows = pl.ds(h * S, S)
        d_scr[rows, :] = jnp.dot(q_ref[0, h], k_ref[0, h], preferred_element_type=jnp.float32) * ATTN_SCALE
        dn_scr[rows, :] = jnp.dot(q_ref[0, h], kn_ref[0, h], preferred_element_type=jnp.float32) * ATTN_SCALE
    d, dn = d_scr[...], dn_scr[...]
    ecs, ens, dens, lses = [], [], [], []
    for i, b_ref in enumerate((b1_ref, b2_ref, b3_ref)):
        w = b_ref.shape[1]
        lc = d[:, P - w:] + b_ref[...]
        ln = dn + bn_ref[i]
        m = jnp.maximum(jnp.max(lc, axis=1, keepdims=True), jnp.max(ln, axis=1, keepdims=True))
        ecs.append(jnp.exp(lc - m))
        ens.append(jnp.exp(ln - m))
        dens.append(jnp.sum(ecs[i], axis=1, keepdims=True) + jnp.sum(ens[i], axis=1, keepdims=True))
        lses.append(m + jnp.log(dens[i]))
    mm = jnp.maximum(jnp.maximum(lses[0], lses[1]), lses[2])
    ws = [jnp.exp(l - mm) for l in lses]
    wsum = ws[0] + ws[1] + ws[2]
    coef = [ws[i] / (wsum * dens[i]) for i in range(3)]
    w1, w2 = b1_ref.shape[1], b2_ref.shape[1]
    p3 = ecs[2] * coef[2]
    p2 = ecs[1] * coef[1]
    p_scr[:, :P - w2] = p3[:, :P - w2]
    p_scr[:, P - w2:P - w1] = p3[:, P - w2:P - w1] + p2[:, :w2 - w1]
    p_scr[:, P - w1:] = p3[:, P - w1:] + p2[:, w2 - w1:] + ecs[0] * coef[0]
    pn_scr[...] = ens[0] * coef[0] + ens[1] * coef[1] + ens[2] * coef[2]
    lane = lax.broadcasted_iota(jnp.int32, (C, S), 1)
    for h in range(H):
        o_tile = jnp.zeros((C, S), jnp.float32)
        for s in range(S):
            r = h * S + s
            acc = None
            for j in range(P // LANES):
                t = v_ref[0, h, :, j * LANES:(j + 1) * LANES] * p_scr[r:r + 1, j * LANES:(j + 1) * LANES]
                acc = t if acc is None else acc + t
            col = jnp.sum(acc, axis=1, keepdims=True) + jnp.sum(vn_ref[0, h] * pn_scr[r:r + 1, :], axis=1, keepdims=True)
            o_tile = jnp.where(lane == s, col, o_tile)
        o_ref[0, h] = o_tile


def _bias_by_distance(rel_bias, dist, valid):
    onehot = np.eye(N_BUCKETS, dtype=np.float32)[_t5_bucket(np.maximum(dist, 0))]
    b = jnp.dot(onehot, rel_bias.astype(jnp.float32), precision=lax.Precision.HIGHEST)
    return jnp.where(valid[:, None], b, NEG_INF)


def _sample_bias_tables(rel_bias, S, lb):
    tabs, news = [], []
    H = rel_bias.shape[1]
    for window, dil in DILATIONS:
        d = np.arange(window + S + 1)
        vec = _bias_by_distance(rel_bias, d, (d % dil == 0) & (d >= 1) & (d <= window))
        rows = [vec[s + 1:s + 1 + window][::-1] for s in range(S)]
        tabs.append(jnp.transpose(jnp.stack(rows), (2, 0, 1)).reshape(H * S, window))
        dn = (np.arange(S)[:, None] - np.arange(S)[None, :]).reshape(-1)
        bn = _bias_by_distance(rel_bias, dn, (dn >= 0) & (dn % dil == 0) & (dn <= window)).reshape(S, S, H)
        news.append(jnp.transpose(bn, (2, 0, 1)).reshape(H * S, S))
    return tabs, jnp.stack(news)


def _sample_attn(q, k_new, v_new, k_buf_t, v_buf_t, rel_bias):
    B, S, H, C = q.shape
    lb = k_buf_t.shape[3]
    assert lb >= DILATIONS[-1][0] and all(w % LANES == 0 for w, _ in DILATIONS)
    tabs, bn = _sample_bias_tables(rel_bias, S, lb)
    tr = lambda a: jnp.transpose(a, (0, 2, 3, 1))
    qs = pl.BlockSpec((1, H, S, C), lambda b: (b, 0, 0, 0))
    tok = pl.BlockSpec((1, H, C, S), lambda b: (b, 0, 0, 0))
    cache = pl.BlockSpec((1, H, C, lb), lambda b: (b, 0, 0, 0))
    full = lambda a: pl.BlockSpec(a.shape, lambda b: (0,) * a.ndim)
    o = pl.pallas_call(
        _sample_attn_kernel,
        grid=(B,),
        in_specs=[qs, tok, tok, cache, cache, full(tabs[0]), full(tabs[1]), full(tabs[2]), full(bn)],
        out_specs=tok,
        out_shape=jax.ShapeDtypeStruct((B, H, C, S), jnp.float32),
        scratch_shapes=[pltpu.VMEM((H * S, lb), jnp.float32), pltpu.VMEM((H * S, S), jnp.float32),
                        pltpu.VMEM((H * S, lb), jnp.float32), pltpu.VMEM((H * S, S), jnp.float32)],
        compiler_params=pltpu.CompilerParams(dimension_semantics=("parallel",),
                                             vmem_limit_bytes=48 * 1024 * 1024),
        name="sample_attn",
    )(jnp.transpose(q, (0, 2, 1, 3)), tr(k_new), tr(v_new), k_buf_t, v_buf_t, *tabs, bn)
    return jnp.transpose(o, (0, 3, 1, 2))


def _prompt_attn_kernel(q_ref, kp_ref, kc_ref, vp_ref, vc_ref, bias_ref, o_ref, lse_ref):
    g = pl.program_id(2)
    n = q_ref.shape[1]
    heads = bias_ref.shape[0]
    lane = lax.broadcasted_iota(jnp.int32, (n, LANES), 1)
    nt = (((1,), (1,)), ((), ()))
    per = LANES // HEAD_DIM
    sls = [slice(hp * LANES, (hp + 1) * LANES) for hp in range(heads // per)]
    keeps = [(lane >= half * HEAD_DIM) & (lane < (half + 1) * HEAD_DIM) for half in range(per)]
    logits = []
    for h in range(heads):
        sl, keep = sls[h // per], keeps[h % per]
        qh = jnp.where(keep, q_ref[0, :, sl], 0.0)
        lp = lax.dot_general(qh, kp_ref[0, :, sl], nt, preferred_element_type=jnp.float32) * ATTN_SCALE + bias_ref[h, :, :n]
        lc = lax.dot_general(qh, kc_ref[0, :, sl], nt, preferred_element_type=jnp.float32) * ATTN_SCALE + bias_ref[h, :, n:]
        logits.append((jnp.where(g == 0, NEG_INF, lp), lc))
    probs = []
    for lp, lc in logits:
        m = jnp.max(jnp.maximum(lp, lc), axis=-1, keepdims=True)
        ep = jnp.exp(lp - m)
        ec = jnp.exp(lc - m)
        s = jnp.sum(ep + ec, axis=-1, keepdims=True)
        probs.append((ep, ec, s, m + jnp.log(s)))
    for hp, sl in enumerate(sls):
        o2 = jnp.zeros((n, LANES), jnp.float32)
        l2 = jnp.zeros((n, LANES), jnp.float32)
        for half in range(per):
            ep, ec, s, lse = probs[hp * per + half]
            o = (jnp.dot(ep, vp_ref[0, :, sl], preferred_element_type=jnp.float32)
                 + jnp.dot(ec, vc_ref[0, :, sl], preferred_element_type=jnp.float32)) / s
            o2 = jnp.where(keeps[half], o, o2)
            l2 = jnp.where(keeps[half], lse, l2)
        o_ref[0, :, sl] = o2
        lse_ref[0, :, sl] = l2


def _merge_kernel(o1, o2, o3, l1, l2, l3, out):
    m = jnp.maximum(jnp.maximum(l1[...], l2[...]), l3[...])
    w1, w2, w3 = jnp.exp(l1[...] - m), jnp.exp(l2[...] - m), jnp.exp(l3[...] - m)
    ws = w1 + w2 + w3
    out[...] = (w1 / ws) * o1[...] + (w2 / ws) * o2[...] + (w3 / ws) * o3[...]


def _prompt_bias(rel_bias, n, dil):
    H = rel_bias.shape[1]
    m = 3 * n
    j = 2 * n - 1 - np.arange(m)
    u = _bias_by_distance(rel_bias, j * dil, (j >= 0) & (j <= n))
    rows = jnp.tile(u, (n, 1))[:n * (m - 1)].reshape(n, m - 1, H)
    return jnp.transpose(rows[:, n - 1:3 * n - 1], (2, 0, 1))


def _prompt_attn(q, k, v, rel_bias):
    B, T, HC = q.shape
    H = HC // HEAD_DIM
    outs, lses = [], []
    for window, dil in DILATIONS:
        n = window // dil
        assert T % window == 0 and HC % LANES == 0
        G = T // window
        view = lambda a: a.reshape(B, T // dil, dil * HC)
        cur = pl.BlockSpec((1, n, HC), lambda b, r, g: (b, g, r))
        prev = pl.BlockSpec((1, n, HC), lambda b, r, g: (b, jnp.maximum(g - 1, 0), r))
        bias = _prompt_bias(rel_bias, n, dil)
        o, lse = pl.pallas_call(
            _prompt_attn_kernel,
            grid=(B, dil, G),
            in_specs=[cur, prev, cur, prev, cur, pl.BlockSpec((H, n, 2 * n), lambda b, r, g: (0, 0, 0))],
            out_specs=[cur, cur],
            out_shape=[jax.ShapeDtypeStruct((B, T // dil, dil * HC), jnp.float32)] * 2,
            compiler_params=pltpu.CompilerParams(dimension_semantics=("parallel", "parallel", "arbitrary")),
            name="prompt_attn_d%d" % dil,
        )(view(q), view(k), view(k), view(v), view(v), bias)
        outs.append(o.reshape(B, T, HC))
        lses.append(lse.reshape(B, T, HC))
    tb = 512
    blk = pl.BlockSpec((1, tb, HC), lambda b, t: (b, t, 0))
    return pl.pallas_call(
        _merge_kernel, grid=(B, T // tb), in_specs=[blk] * 6, out_specs=blk,
        out_shape=jax.ShapeDtypeStruct((B, T, HC), jnp.float32),
        compiler_params=pltpu.CompilerParams(dimension_semantics=("parallel", "parallel")),
        name="prompt_attn_merge",
    )(*outs, *lses)


WKV_ROWS_PER_PASS = 4


def _wkv_kernel(w_ref, nkk_ref, b_ref, k_ref, r_ref, vop_ref, s0_ref, y_ref, s_ref, *, tt, vp, kd):
    j = pl.program_id(1)
    kgs = kd // SUBLANES
    G = WKV_ROWS_PER_PASS

    @pl.when(j == 0)
    def _():
        s_ref[...] = s0_ref[...]

    def step(t, c):
        for v0 in range(0, vp, G):
            accs = [None] * G
            for kg in range(kgs):
                ks = pl.ds(kg * SUBLANES, SUBLANES)
                nkk = nkk_ref[t, ks, :]
                for i in range(G):
                    p = s_ref[v0 + i, ks, :] * nkk
                    accs[i] = p if accs[i] is None else accs[i] + p
            sa = [jnp.sum(a, axis=0, keepdims=True) for a in accs]
            vv = [vop_ref[t, pl.ds(v0 + i, 1), :] for i in range(G)]
            yacc = [None] * G
            for kg in range(kgs):
                ks = pl.ds(kg * SUBLANES, SUBLANES)
                w = w_ref[t, ks, :]
                b = b_ref[t, ks, :]
                k = k_ref[t, ks, :]
                r = r_ref[t, ks, :]
                for i in range(G):
                    s2 = s_ref[v0 + i, ks, :] * w + sa[i] * b + vv[i] * k
                    s_ref[v0 + i, ks, :] = s2
                    p = s2 * r
                    yacc[i] = p if yacc[i] is None else yacc[i] + p
            for i in range(G):
                y_ref[t, pl.ds(v0 + i, 1), :] = jnp.sum(yacc[i], axis=0, keepdims=True)
        return c

    lax.fori_loop(0, tt, step, 0)


def _wkv_scan(r, w, k, v, kk, a, s0):
    B, T, H, N = r.shape
    bh = B * H
    nvh = max(1, LANES // bh)
    L = nvh * bh
    assert L % LANES == 0 and N % (nvh * WKV_ROWS_PER_PASS) == 0 and N % SUBLANES == 0
    vp = N // nvh
    tt = T if T <= 32 else 32
    assert T % tt == 0

    def kform(x):
        x = jnp.transpose(x, (1, 3, 0, 2)).reshape(T, N, 1, bh)
        return jnp.broadcast_to(x, (T, N, nvh, bh)).reshape(T, N, L)

    kops = [kform(w), kform(-kk), kform(kk * a), kform(k), kform(r)]
    vop = jnp.transpose(v, (1, 3, 0, 2)).reshape(T, nvh, vp, bh)
    vop = jnp.transpose(vop, (0, 2, 1, 3)).reshape(T, vp, L)
    s0t = jnp.transpose(s0, (2, 3, 0, 1)).reshape(nvh, vp, N, bh)
    s0t = jnp.transpose(s0t, (1, 2, 0, 3)).reshape(vp, N, L)
    kern = functools.partial(_wkv_kernel, tt=tt, vp=vp, kd=N)
    y, sf = pl.pallas_call(
        kern,
        grid=(L // LANES, T // tt),
        in_specs=[pl.BlockSpec((tt, N, LANES), lambda l, j: (j, 0, l))] * 5 + [
                  pl.BlockSpec((tt, vp, LANES), lambda l, j: (j, 0, l)),
                  pl.BlockSpec((vp, N, LANES), lambda l, j: (0, 0, l))],
        out_specs=[pl.BlockSpec((tt, vp, LANES), lambda l, j: (j, 0, l)),
                   pl.BlockSpec((vp, N, LANES), lambda l, j: (0, 0, l))],
        out_shape=[jax.ShapeDtypeStruct((T, vp, L), jnp.float32),
                   jax.ShapeDtypeStruct((vp, N, L), jnp.float32)],
        compiler_params=pltpu.CompilerParams(dimension_semantics=("parallel", "arbitrary"),
                                             vmem_limit_bytes=48 * 1024 * 1024),
        name="wkv_scan",
    )(*kops, vop, s0t)
    y = jnp.transpose(y.reshape(T, vp, nvh, B, H), (3, 0, 4, 2, 1)).reshape(B, T, H, N)
    sf = jnp.transpose(sf.reshape(vp, N, nvh, B, H), (3, 4, 2, 0, 1)).reshape(B, H, N, N)
    return y, sf


def _rwkv_mixer(pb, shift0, s0, mu, w0, w_w2, a0, w_a2, w_g2, k_k, k_a, r_k, lnx_g, lnx_b):
    B, T, _ = pb.shape
    pb = pb.astype(jnp.float32)
    prev = jnp.concatenate([shift0[:, None, :].astype(jnp.float32), pb[:, :-1]], axis=1)
    xm = pb + (prev - pb) * mu
    c = C_RWKV
    r, k, v = xm[..., :c], xm[..., c:2 * c], xm[..., 2 * c:3 * c]
    o = 3 * c
    wl = xm[..., o:o + LORA_DECAY]
    al = xm[..., o + LORA_DECAY:o + LORA_DECAY + LORA_ICLR]
    gl = xm[..., o + LORA_DECAY + LORA_ICLR:]
    w_log = -jax.nn.softplus(-(w0 + jnp.tanh(wl) @ w_w2)) - 0.5
    decay = jnp.exp(-jnp.exp(w_log))
    a = jax.nn.sigmoid(a0 + al @ w_a2)
    g = jax.nn.sigmoid(gl) @ w_g2

    def heads(t):
        return t.reshape(B, T, N_HEADS_RWKV, HEAD_DIM)
    kk = heads(k * k_k)
    kk = kk / jnp.maximum(jnp.sqrt(jnp.sum(kk * kk, axis=-1, keepdims=True)), 1e-12)
    k = k * (1.0 + (a - 1.0) * k_a)
    r_h, k_h, v_h = heads(r), heads(k), heads(v)
    y, s_fin = _wkv_scan(r_h, heads(decay), k_h, v_h, kk, heads(a), s0.astype(jnp.float32))
    mean = jnp.mean(y, axis=-1, keepdims=True)
    var = jnp.mean(jnp.square(y - mean), axis=-1, keepdims=True)
    y = ((y - mean) * lax.rsqrt(var + GN_EPS)).reshape(B, T, C_RWKV) * lnx_g + lnx_b
    bonus = jnp.sum(r_h * k_h * r_k, axis=-1, keepdims=True) * v_h
    y = (y + bonus.reshape(B, T, C_RWKV)) * g
    return y, s_fin, pb[:, -1]


def _gelu_exact(x):
    return 0.5 * x * (1.0 + lax.erf(x * (2.0 ** -0.5)))


PEER_TOK = 8


def _peer_expert_kernel(eidc_ref, eidn_ref, h_ref, gate_ref, tab_ref, out_ref, buf, sem, *, rows, dim):
    i = pl.program_id(0)
    n = pl.num_programs(0)
    tok = PEER_TOK
    chunks = dim // LANES
    erow = chunks
    groups = rows // SUBLANES
    slot_groups = tok * groups

    def start_row_copy(eid_ref, src_t, t, r, slot):
        e = eid_ref[src_t, r]
        src = tab_ref.at[pl.ds(pl.multiple_of(e * erow, erow), erow)]
        dst = buf.at[slot * slot_groups + t * groups + r // SUBLANES, :, r % SUBLANES, :]
        pltpu.make_async_copy(src, dst, sem.at[slot]).start(priority=r % 2)

    def wait_slot(slot):
        region = buf.at[pl.ds(slot * slot_groups, slot_groups)]
        pltpu.make_async_copy(region, region, sem.at[slot]).wait()

    def compute(t_blk, t, slot):
        g0 = slot * slot_groups + t * groups
        hb = [jnp.broadcast_to(h_ref[t_blk:t_blk + 1, k * LANES:(k + 1) * LANES], (SUBLANES, LANES))
              for k in range(chunks)]
        lane = lax.broadcasted_iota(jnp.int32, (SUBLANES, LANES), 1)
        hi_mask = jnp.uint32(0xFFFF0000)
        s_tile = jnp.zeros((SUBLANES, LANES), jnp.float32)
        for g in range(groups):
            acc = None
            for k in range(chunks):
                p = pltpu.bitcast(buf[g0 + g, k] & hi_mask, jnp.float32) * hb[k]
                acc = p if acc is None else acc + p
            s_tile = jnp.where(lane == g, jnp.sum(acc, axis=1, keepdims=True), s_tile)
        w_tile = _gelu_exact(s_tile) * gate_ref[t_blk]
        accs = [None] * chunks
        for g in range(groups):
            wg = jnp.broadcast_to(w_tile[:, g:g + 1], (SUBLANES, LANES))
            for k in range(chunks):
                p = wg * pltpu.bitcast(buf[g0 + g, k] << 16, jnp.float32)
                accs[k] = p if accs[k] is None else accs[k] + p
        out_ref[t_blk:t_blk + 1, :] = jnp.concatenate([jnp.sum(a, axis=0, keepdims=True) for a in accs], axis=1)

    @pl.when(i == 0)
    def _():
        for t in range(tok):
            for r in range(rows):
                start_row_copy(eidc_ref, t, t, r, 0)

    wait_slot(0)
    for t in range(tok):
        for r in range(rows):
            start_row_copy(eidc_ref, tok + t, t, r, 1)
        compute(t, t, 0)
    wait_slot(1)
    for t in range(tok):
        for r in range(rows):
            start_row_copy(eidn_ref, t, t, r, 0)
        compute(tok + t, t, 1)

    @pl.when(i == n - 1)
    def _():
        wait_slot(0)


def peer_experts(h, eid, gate_tile, expert_u, expert_v):
    N, D = h.shape
    R = eid.shape[1]
    E = expert_u.shape[0]
    chunks = D // LANES
    step_tok = 2 * PEER_TOK
    assert N % step_tok == 0 and R % SUBLANES == 0 and D % LANES == 0
    groups = R // SUBLANES

    def bf16_bits(x):
        return lax.bitcast_convert_type(x.astype(jnp.bfloat16), jnp.uint16).astype(jnp.uint32)
    tab = ((bf16_bits(expert_u) << 16) | bf16_bits(expert_v)).reshape(E * chunks, LANES)
    nsteps = N // step_tok
    kern = functools.partial(_peer_expert_kernel, rows=R, dim=D)
    return pl.pallas_call(
        kern,
        grid=(nsteps,),
        in_specs=[
            pl.BlockSpec((step_tok, R), lambda i: (i, 0), memory_space=pltpu.SMEM),
            pl.BlockSpec((step_tok, R), lambda i: (jnp.minimum(i + 1, nsteps - 1), 0), memory_space=pltpu.SMEM),
            pl.BlockSpec((step_tok, D), lambda i: (i, 0)),
            pl.BlockSpec((step_tok, SUBLANES, LANES), lambda i: (i, 0, 0)),
            pl.BlockSpec(memory_space=pl.ANY),
        ],
        out_specs=pl.BlockSpec((step_tok, D), lambda i: (i, 0)),
        out_shape=jax.ShapeDtypeStruct((N, D), jnp.float32),
        scratch_shapes=[pltpu.VMEM((2 * PEER_TOK * groups, chunks, SUBLANES, LANES), jnp.uint32),
                        pltpu.SemaphoreType.DMA((2,))],
        compiler_params=pltpu.CompilerParams(dimension_semantics=("arbitrary",),
                                             vmem_limit_bytes=48 * 1024 * 1024,
                                             disable_bounds_checks=True),
        name="peer_experts",
    )(eid, eid, h, gate_tile, tab)


PEER_PAIRS = tuple((a, b) for a in range(PEER_TOPK) for b in range(PEER_TOPK) if (a + 1) * (b + 1) <= PEER_TOPK)


def _top_rows(x, pos, k):
    vals, idxs = [], []
    for _ in range(k):
        m = jnp.max(x, axis=0, keepdims=True)
        i = jnp.min(jnp.where(x == m, pos, jnp.float32(1e9)), axis=0, keepdims=True)
        vals.append(m)
        idxs.append(i)
        x = jnp.where(pos == i, -jnp.inf, x)
    return vals, idxs


def _joint_topk(s_ref, eid_ref, gate_ref, *, heads, keys, topk):
    tb = s_ref.shape[1]
    key_pos = lax.broadcasted_iota(jnp.int32, (keys, tb), 0).astype(jnp.float32)
    nrow = -(-len(PEER_PAIRS) // SUBLANES) * SUBLANES
    pair_pos = lax.broadcasted_iota(jnp.int32, (nrow, tb), 0).astype(jnp.float32)
    for h in range(heads):
        v1, i1 = _top_rows(s_ref[pl.ds((2 * h) * keys, keys), :], key_pos, topk)
        v2, i2 = _top_rows(s_ref[pl.ds((2 * h + 1) * keys, keys), :], key_pos, topk)
        cand = jnp.full((nrow, tb), -jnp.inf, jnp.float32)
        cid = jnp.zeros((nrow, tb), jnp.float32)
        for p, (a, b) in enumerate(PEER_PAIRS):
            cand = jnp.where(pair_pos == p, v1[a] + v2[b], cand)
            cid = jnp.where(pair_pos == p, i1[a] * keys + i2[b], cid)
        top_s, top_p = _top_rows(cand, pair_pos, topk)
        es = [jnp.exp(s - top_s[0]) for s in top_s]
        den = es[0]
        for e in es[1:]:
            den = den + e
        for r in range(topk):
            eid = jnp.sum(jnp.where(pair_pos == top_p[r], cid, 0.0), axis=0, keepdims=True)
            eid_ref[pl.ds(h * topk + r, 1), :] = eid.astype(jnp.int32)
            gate_ref[pl.ds(h * topk + r, 1), :] = es[r] / den


def _route_kernel(h_ref, wq_ref, sk_ref, eid_ref, gate_ref, s_scr, *, heads, keys, topk):
    q = jnp.dot(h_ref[...].astype(jnp.bfloat16), wq_ref[...], preferred_element_type=jnp.float32)
    half = sk_ref.shape[2]
    nt = (((1,), (1,)), ((), ()))
    for g in range(2 * heads):
        qg = q[:, g * half:(g + 1) * half].astype(jnp.bfloat16)
        s_scr[pl.ds(g * keys, keys), :] = lax.dot_general(sk_ref[g], qg, nt, preferred_element_type=jnp.float32)
    _joint_topk(s_scr, eid_ref, gate_ref, heads=heads, keys=keys, topk=topk)


def _peer_route(h, w_pq, sub_keys, *, tb=LANES):
    n_tok, D = h.shape
    heads, _, keys, half = sub_keys.shape
    assert n_tok % tb == 0
    wq = w_pq.astype(jnp.bfloat16)
    sk = sub_keys.reshape(heads * 2, keys, half).astype(jnp.bfloat16)
    kern = functools.partial(_route_kernel, heads=heads, keys=keys, topk=PEER_TOPK)
    const = lambda a: pl.BlockSpec(a.shape, lambda i: (0,) * a.ndim)
    eid_t, gate_t = pl.pallas_call(
        kern,
        grid=(n_tok // tb,),
        in_specs=[pl.BlockSpec((tb, D), lambda i: (i, 0)), const(wq), const(sk)],
        out_specs=[pl.BlockSpec((heads * PEER_TOPK, tb), lambda i: (0, i)),
                   pl.BlockSpec((heads * PEER_TOPK, tb), lambda i: (0, i))],
        out_shape=[jax.ShapeDtypeStruct((heads * PEER_TOPK, n_tok), jnp.int32),
                   jax.ShapeDtypeStruct((heads * PEER_TOPK, n_tok), jnp.float32)],
        scratch_shapes=[pltpu.VMEM((heads * 2 * keys, tb), jnp.float32)],
        compiler_params=pltpu.CompilerParams(dimension_semantics=("parallel",),
                                             vmem_limit_bytes=48 * 1024 * 1024),
        name="peer_route",
    )(h, wq, sk)
    groups = PEER_HEADS * PEER_TOPK // SUBLANES
    gate_tile = jnp.transpose(gate_t.reshape(groups, SUBLANES, n_tok), (2, 1, 0))
    gate_tile = jnp.pad(gate_tile, ((0, 0), (0, 0), (0, LANES - groups)))
    return eid_t.T, gate_tile


def _mix_proj_kernel(oa_ref, yr_ref, x_ref, g1_ref, sc_ref, sh_ref, gn_ref, wa_ref, wr_ref, x1_ref, h2_ref):
    mix = (jnp.dot(oa_ref[0].astype(jnp.bfloat16), wa_ref[...], preferred_element_type=jnp.float32)
           + jnp.dot(yr_ref[0].astype(jnp.bfloat16), wr_ref[...], preferred_element_type=jnp.float32))
    x1 = x_ref[0] + g1_ref[0] * mix
    x1_ref[0] = x1
    h = x1 * lax.rsqrt(jnp.mean(x1 * x1, axis=-1, keepdims=True) + NORM_EPS) * gn_ref[...]
    h2_ref[0] = h * (1.0 + sc_ref[0]) + sh_ref[0]


def _mix_proj(o_attn, y_rwkv, x, g1, sc2, sh2, gain2, w_out):
    B, T, D = x.shape
    ca, cr = o_attn.shape[2], y_rwkv.shape[2]
    tm = min(T, 512)
    assert T % tm == 0
    blk = lambda w: pl.BlockSpec((1, tm, w), lambda b, t: (b, t, 0))
    mod = _mod_spec(g1, tm)
    const = lambda a: pl.BlockSpec(a.shape, lambda b, t: (0,) * a.ndim)
    wa, wr = w_out[:ca].astype(jnp.bfloat16), w_out[ca:].astype(jnp.bfloat16)
    gn = gain2.reshape(1, D)
    return pl.pallas_call(
        _mix_proj_kernel,
        grid=(B, T // tm),
        in_specs=[blk(ca), blk(cr), blk(D), mod, mod, mod, const(gn), const(wa), const(wr)],
        out_specs=[blk(D), blk(D)],
        out_shape=[jax.ShapeDtypeStruct((B, T, D), jnp.float32)] * 2,
        compiler_params=pltpu.CompilerParams(dimension_semantics=("parallel", "parallel"),
                                             vmem_limit_bytes=48 * 1024 * 1024),
        name="mix_proj",
    )(o_attn, y_rwkv, x, g1, sc2, sh2, gn, wa, wr)


def _resid_kernel(x_ref, g_ref, f_ref, o_ref):
    o_ref[...] = x_ref[...] + g_ref[...] * f_ref[...]


def _resid(x, g, f):
    B, T, D = x.shape
    tb = min(T, 512)
    return pl.pallas_call(
        _resid_kernel,
        grid=(B, T // tb),
        in_specs=[pl.BlockSpec((1, tb, D), lambda b, t: (b, t, 0)),
                  pl.BlockSpec((1, 1, D), lambda b, t: (b, 0, 0)),
                  pl.BlockSpec((1, tb, D), lambda b, t: (b, t, 0))],
        out_specs=pl.BlockSpec((1, tb, D), lambda b, t: (b, t, 0)),
        out_shape=jax.ShapeDtypeStruct(x.shape, x.dtype),
        name="ffn_residual",
    )(x, g, f)


def _layer(x, c, k_buf, v_buf, wkv0, shift0, rel_bias, p):
    B, T, _ = x.shape
    mod = jax.nn.silu(c.astype(jnp.float32)) @ p['ada_w'] + p['ada_b']
    sh1, sc1, g1, sh2, sc2, g2 = jnp.split(mod[:, None, :], 6, axis=-1)
    if T < SUBLANES:
        fold = lambda a: a.reshape(1, B * T, a.shape[-1])
        per_token = lambda m: fold(jnp.broadcast_to(m, (B, T, D_MODEL)))
    else:
        fold = per_token = lambda a: a
    unfold = lambda a: a.reshape(B, T, a.shape[-1])
    q3, k3, v3, proj_rwkv = (unfold(a) for a in _in_proj(
        fold(x), p['norm1_g'], per_token(sc1), per_token(sh1), p['w_in'], p['q_norm_g'], p['k_norm_g']))
    q, k, v = (a.reshape(B, T, N_HEADS_ATTN, HEAD_DIM) for a in (q3, k3, v3))
    if k_buf is None:
        o_attn = _prompt_attn(q3, k3, v3, rel_bias)
        keep = min(MAX_WINDOW, T)
        k_rows, v_rows = k[:, T - keep:], v[:, T - keep:]
        wkv0 = jnp.zeros((B, N_HEADS_RWKV, HEAD_DIM, HEAD_DIM), jnp.float32)
        shift0 = jnp.zeros((B, COLS_RWKV), jnp.float32)
    else:
        o_attn = _sample_attn(q, k, v, jnp.transpose(k_buf, (0, 2, 3, 1)), jnp.transpose(v_buf, (0, 2, 3, 1)), rel_bias)
        k_rows, v_rows = k, v
    y_rwkv, s_fin, shift_new = _rwkv_mixer(
        proj_rwkv, shift0, wkv0, p['mu_shift'], p['w0'], p['w_w2'], p['a0'], p['w_a2'],
        p['w_g2'], p['k_k'], p['k_a'], p['r_k'], p['lnx_g'], p['lnx_b'])
    x, h2 = (unfold(a) for a in _mix_proj(
        fold(o_attn.reshape(B, T, C_ATTN)), fold(y_rwkv), fold(x), per_token(g1), per_token(sc2), per_token(sh2),
        p['norm2_g'], p['w_out']))
    return x, h2, g2, k_rows, v_rows, s_fin, shift_new


def kernel(x_prompt, x_sample, c_prompt, c_sample, cache_k_win, cache_v_win, state_wkv, state_shift,
           ada_w, ada_b, norm1_g, norm2_g, w_in, q_norm_g, k_norm_g, rel_bias, mu_shift, w0, w_w2, a0,
           w_a2, w_g2, k_k, k_a, r_k, lnx_g, lnx_b, w_out, w_peer_q, peer_sub_keys, expert_u, expert_v):
    xp, xs = x_prompt, x_sample
    kp_l, vp_l, sp_l, hp_l = [], [], [], []
    ks_l, vs_l, ss_l, hs_l = [], [], [], []
    names = ('ada_w', 'ada_b', 'norm1_g', 'norm2_g', 'w_in', 'q_norm_g', 'k_norm_g', 'mu_shift', 'w0', 'w_w2',
             'a0', 'w_a2', 'w_g2', 'k_k', 'k_a', 'r_k', 'lnx_g', 'lnx_b', 'w_out', 'w_peer_q', 'peer_sub_keys',
             'expert_u', 'expert_v')
    vals = (ada_w, ada_b, norm1_g, norm2_g, w_in, q_norm_g, k_norm_g, mu_shift, w0, w_w2, a0, w_a2, w_g2, k_k,
            k_a, r_k, lnx_g, lnx_b, w_out, w_peer_q, peer_sub_keys, expert_u, expert_v)
    for l in range(DEPTH):
        p = {n: v[l] for n, v in zip(names, vals)}
        xp, h2p, g2p, kp, vp, sp, hp = _layer(xp, c_prompt, None, None, None, None, rel_bias, p)
        xs, h2s, g2s, kn, vn, sn, hn = _layer(xs, c_sample, cache_k_win[l], cache_v_win[l], state_wkv[l],
                                              state_shift[l], rel_bias, p)
        n_p = h2p.shape[0] * h2p.shape[1]
        h2 = jnp.concatenate([h2p.reshape(-1, D_MODEL), h2s.reshape(-1, D_MODEL)], axis=0)
        eid, gate_tile = _peer_route(h2, p['w_peer_q'], p['peer_sub_keys'])
        ffn = peer_experts(h2, eid, gate_tile, p['expert_u'], p['expert_v'])
        xp = _resid(xp, g2p, ffn[:n_p].reshape(xp.shape))
        xs = _resid(xs, g2s, ffn[n_p:].reshape(xs.shape))
        kp_l.append(kp); vp_l.append(vp); sp_l.append(sp); hp_l.append(hp)
        ks_l.append(kn); vs_l.append(vn); ss_l.append(sn); hs_l.append(hn)
    return (xp, xs, jnp.stack(kp_l), jnp.stack(vp_l), jnp.stack(sp_l), jnp.stack(hp_l),
            jnp.stack(ks_l), jnp.stack(vs_l), jnp.stack(ss_l), jnp.stack(hs_l))
```

```python
import functools
import math
import jax, jax.numpy as jnp
from jax import lax
import numpy as np
from jax.experimental import pallas as pl
from jax.experimental.pallas import tpu as pltpu

D_MODEL = 1024
DEPTH = 1
HEAD_DIM = 64
N_HEADS_ATTN = 8
N_HEADS_RWKV = 8
C_ATTN = N_HEADS_ATTN * HEAD_DIM
C_RWKV = N_HEADS_RWKV * HEAD_DIM
DILATIONS = ((128, 1), (512, 4), (2048, 16))
MAX_WINDOW = 2048
N_BUCKETS = 32
MAX_DISTANCE = 2048
LORA_DECAY = 32
LORA_ICLR = 32
LORA_GATE = 64
COLS_RWKV = 3 * C_RWKV + LORA_DECAY + LORA_ICLR + LORA_GATE
D_IN = 3 * C_ATTN + COLS_RWKV
PEER_HEADS = 8
PEER_KEYS = 128
PEER_QDIM = 256
PEER_HALF = PEER_QDIM // 2
PEER_TOPK = 16
PEER_CHUNK = 256
NORM_EPS = 1e-6
GN_EPS = 64e-5
NEG_INF = -1e30
ATTN_SCALE = HEAD_DIM ** -0.5


def _rms(x, g):
    x32 = x.astype(jnp.float32)
    return x32 * lax.rsqrt(jnp.mean(x32 * x32, axis=-1, keepdims=True) + NORM_EPS) * g


def _t5_bucket(dist):
    dist = np.asarray(dist, dtype=np.int64)
    max_exact = N_BUCKETS // 2
    safe = np.maximum(dist, 1) / max_exact
    large = max_exact + (np.log(safe) / math.log(MAX_DISTANCE / max_exact) * (N_BUCKETS - max_exact)).astype(np.int64)
    large = np.minimum(large, N_BUCKETS - 1)
    return np.where(dist < max_exact, dist, large).astype(np.int32)


LANES = 128
SUBLANES = 8


def _mod_spec(a, tm):
    per_token = a.shape[1] != 1
    return pl.BlockSpec((1, tm if per_token else 1, a.shape[2]), lambda b, t: (b, t if per_token else 0, 0))


def _segment_mean_matrix(width, seg):
    lane = np.arange(width)
    return jnp.asarray((lane[:, None] // seg == lane[None, :] // seg).astype(np.float32) / seg, jnp.bfloat16)


def _head_rms(x, seg_ref, gain):
    x2 = x * x
    hi = x2.astype(jnp.bfloat16)
    lo = (x2 - hi.astype(jnp.float32)).astype(jnp.bfloat16)
    ms = (jnp.dot(hi, seg_ref[...], preferred_element_type=jnp.float32)
          + jnp.dot(lo, seg_ref[...], preferred_element_type=jnp.float32))
    return x * lax.rsqrt(ms + NORM_EPS) * gain


def _in_proj_kernel(x_ref, g_ref, sc_ref, sh_ref, w_ref, seg_ref, gq_ref, gk_ref, q_ref, k_ref, v_ref, r_ref):
    x = x_ref[0]
    h = x * lax.rsqrt(jnp.mean(x * x, axis=-1, keepdims=True) + NORM_EPS) * g_ref[...]
    h = (h * (1.0 + sc_ref[0]) + sh_ref[0]).astype(jnp.bfloat16)
    c = q_ref.shape[2]
    q_ref[0] = _head_rms(jnp.dot(h, w_ref[:, :c], preferred_element_type=jnp.float32), seg_ref, gq_ref[...])
    k_ref[0] = _head_rms(jnp.dot(h, w_ref[:, c:2 * c], preferred_element_type=jnp.float32), seg_ref, gk_ref[...])
    v_ref[0] = jnp.dot(h, w_ref[:, 2 * c:3 * c], preferred_element_type=jnp.float32)
    r_ref[0] = jnp.dot(h, w_ref[:, 3 * c:], preferred_element_type=jnp.float32)


def _in_proj(x, gain, sc, sh, w_in, q_gain, k_gain):
    B, T, D = x.shape
    n_out = w_in.shape[1]
    c = C_ATTN
    tm = min(T, 512)
    assert T % tm == 0
    heads = c // HEAD_DIM
    row = lambda a: a.reshape(1, -1)
    blk = lambda w: pl.BlockSpec((1, tm, w), lambda b, t: (b, t, 0))
    const = lambda a: pl.BlockSpec(a.shape, lambda b, t: (0,) * a.ndim)
    mod = _mod_spec(sc, tm)
    seg = _segment_mean_matrix(c, HEAD_DIM)
    wb = w_in.astype(jnp.bfloat16)
    gq, gk = row(jnp.tile(q_gain, heads)), row(jnp.tile(k_gain, heads))
    return pl.pallas_call(
        _in_proj_kernel,
        grid=(B, T // tm),
        in_specs=[blk(D), const(row(gain)), mod, mod, const(wb), const(seg), const(gq), const(gk)],
        out_specs=[blk(c), blk(c), blk(c), blk(n_out - 3 * c)],
        out_shape=[jax.ShapeDtypeStruct((B, T, c), jnp.float32)] * 3
                  + [jax.ShapeDtypeStruct((B, T, n_out - 3 * c), jnp.float32)],
        compiler_params=pltpu.CompilerParams(dimension_semantics=("parallel", "parallel"),
                                             vmem_limit_bytes=56 * 1024 * 1024),
        name="in_proj",
    )(x, row(gain), sc, sh, wb, seg, gq, gk)


def _sample_attn_kernel(q_ref, kn_ref, vn_ref, k_ref, v_ref, b1_ref, b2_ref, b3_ref, bn_ref, o_ref,
                        d_scr, dn_scr, p_scr, pn_scr):
    _, H, S, C = q_ref.shape
    P = k_ref.shape[3]
    for h in range(H):
        rows = pl.ds(h * S, S)
        d_scr[rows, :] = jnp.dot(q_ref[0, h], k_ref[0, h], preferred_element_type=jnp.float32) * ATTN_SCALE
        dn_scr[rows, :] = jnp.dot(q_ref[0, h], kn_ref[0, h], preferred_element_type=jnp.float32) * ATTN_SCALE
    d, dn = d_scr[...], dn_scr[...]
    ecs, ens, dens, lses = [], [], [], []
    for i, b_ref in enumerate((b1_ref, b2_ref, b3_ref)):
        w = b_ref.shape[1]
        lc = d[:, P - w:] + b_ref[...]
        ln = dn + bn_ref[i]
        m = jnp.maximum(jnp.max(lc, axis=1, keepdims=True), jnp.max(ln, axis=1, keepdims=True))
        ecs.append(jnp.exp(lc - m))
        ens.append(jnp.exp(ln - m))
        dens.append(jnp.sum(ecs[i], axis=1, keepdims=True) + jnp.sum(ens[i], axis=1, keepdims=True))
        lses.append(m + jnp.log(dens[i]))
    mm = jnp.maximum(jnp.maximum(lses[0], lses[1]), lses[2])
    ws = [jnp.exp(l - mm) for l in lses]
    wsum = ws[0] + ws[1] + ws[2]
    coef = [ws[i] / (wsum * dens[i]) for i in range(3)]
    w1, w2 = b1_ref.shape[1], b2_ref.shape[1]
    p3 = ecs[2] * coef[2]
    p2 = ecs[1] * coef[1]
    p_scr[:, :P - w2] = p3[:, :P - w2]
    p_scr[:, P - w2:P - w1] = p3[:, P - w2:P - w1] + p2[:, :w2 - w1]
    p_scr[:, P - w1:] = p3[:, P - w1:] + p2[:, w2 - w1:] + ecs[0] * coef[0]
    pn_scr[...] = ens[0] * coef[0] + ens[1] * coef[1] + ens[2] * coef[2]
    lane = lax.broadcasted_iota(jnp.int32, (C, S), 1)
    for h in range(H):
        o_tile = jnp.zeros((C, S), jnp.float32)
        for s in range(S):
            r = h * S + s
            acc = None
            for j in range(P // LANES):
                t = v_ref[0, h, :, j * LANES:(j + 1) * LANES] * p_scr[r:r + 1, j * LANES:(j + 1) * LANES]
                acc = t if acc is None else acc + t
            col = jnp.sum(acc, axis=1, keepdims=True) + jnp.sum(vn_ref[0, h] * pn_scr[r:r + 1, :], axis=1, keepdims=True)
            o_tile = jnp.where(lane == s, col, o_tile)
        o_ref[0, h] = o_tile


def _bias_by_distance(rel_bias, dist, valid):
    onehot = np.eye(N_BUCKETS, dtype=np.float32)[_t5_bucket(np.maximum(dist, 0))]
    b = jnp.dot(onehot, rel_bias.astype(jnp.float32), precision=lax.Precision.HIGHEST)
    return jnp.where(valid[:, None], b, NEG_INF)


def _sample_bias_tables(rel_bias, S, lb):
    tabs, news = [], []
    H = rel_bias.shape[1]
    for window, dil in DILATIONS:
        d = np.arange(window + S + 1)
        vec = _bias_by_distance(rel_bias, d, (d % dil == 0) & (d >= 1) & (d <= window))
        rows = [vec[s + 1:s + 1 + window][::-1] for s in range(S)]
        tabs.append(jnp.transpose(jnp.stack(rows), (2, 0, 1)).reshape(H * S, window))
        dn = (np.arange(S)[:, None] - np.arange(S)[None, :]).reshape(-1)
        bn = _bias_by_distance(rel_bias, dn, (dn >= 0) & (dn % dil == 0) & (dn <= window)).reshape(S, S, H)
        news.append(jnp.transpose(bn, (2, 0, 1)).reshape(H * S, S))
    return tabs, jnp.stack(news)


def _sample_attn(q, k_new, v_new, k_buf_t, v_buf_t, rel_bias):
    B, S, H, C = q.shape
    lb = k_buf_t.shape[3]
    assert lb >= DILATIONS[-1][0] and all(w % LANES == 0 for w, _ in DILATIONS)
    tabs, bn = _sample_bias_tables(rel_bias, S, lb)
    tr = lambda a: jnp.transpose(a, (0, 2, 3, 1))
    qs = pl.BlockSpec((1, H, S, C), lambda b: (b, 0, 0, 0))
    tok = pl.BlockSpec((1, H, C, S), lambda b: (b, 0, 0, 0))
    cache = pl.BlockSpec((1, H, C, lb), lambda b: (b, 0, 0, 0))
    full = lambda a: pl.BlockSpec(a.shape, lambda b: (0,) * a.ndim)
    o = pl.pallas_call(
        _sample_attn_kernel,
        grid=(B,),
        in_specs=[qs, tok, tok, cache, cache, full(tabs[0]), full(tabs[1]), full(tabs[2]), full(bn)],
        out_specs=tok,
        out_shape=jax.ShapeDtypeStruct((B, H, C, S), jnp.float32),
        scratch_shapes=[pltpu.VMEM((H * S, lb), jnp.float32), pltpu.VMEM((H * S, S), jnp.float32),
                        pltpu.VMEM((H * S, lb), jnp.float32), pltpu.VMEM((H * S, S), jnp.float32)],
        compiler_params=pltpu.CompilerParams(dimension_semantics=("parallel",),
                                             vmem_limit_bytes=48 * 1024 * 1024),
        name="sample_attn",
    )(jnp.transpose(q, (0, 2, 1, 3)), tr(k_new), tr(v_new), k_buf_t, v_buf_t, *tabs, bn)
    return jnp.transpose(o, (0, 3, 1, 2))


def _prompt_attn_kernel(q_ref, kp_ref, kc_ref, vp_ref, vc_ref, bias_ref, o_ref, lse_ref):
    g = pl.program_id(2)
    n = q_ref.shape[1]
    heads = bias_ref.shape[0]
    lane = lax.broadcasted_iota(jnp.int32, (n, LANES), 1)
    nt = (((1,), (1,)), ((), ()))
    per = LANES // HEAD_DIM
    sls = [slice(hp * LANES, (hp + 1) * LANES) for hp in range(heads // per)]
    keeps = [(lane >= half * HEAD_DIM) & (lane < (half + 1) * HEAD_DIM) for half in range(per)]
    logits = []
    for h in range(heads):
        sl, keep = sls[h // per], keeps[h % per]
        qh = jnp.where(keep, q_ref[0, :, sl], 0.0)
        lp = lax.dot_general(qh, kp_ref[0, :, sl], nt, preferred_element_type=jnp.float32) * ATTN_SCALE + bias_ref[h, :, :n]
        lc = lax.dot_general(qh, kc_ref[0, :, sl], nt, preferred_element_type=jnp.float32) * ATTN_SCALE + bias_ref[h, :, n:]
        logits.append((jnp.where(g == 0, NEG_INF, lp), lc))
    probs = []
    for lp, lc in logits:
        m = jnp.max(jnp.maximum(lp, lc), axis=-1, keepdims=True)
        ep = jnp.exp(lp - m)
        ec = jnp.exp(lc - m)
        s = jnp.sum(ep + ec, axis=-1, keepdims=True)
        probs.append((ep, ec, s, m + jnp.log(s)))
    for hp, sl in enumerate(sls):
        o2 = jnp.zeros((n, LANES), jnp.float32)
        l2 = jnp.zeros((n, LANES), jnp.float32)
        for half in range(per):
            ep, ec, s, lse = probs[hp * per + half]
            o = (jnp.dot(ep, vp_ref[0, :, sl], preferred_element_type=jnp.float32)
                 + jnp.dot(ec, vc_ref[0, :, sl], preferred_element_type=jnp.float32)) / s
            o2 = jnp.where(keeps[half], o, o2)
            l2 = jnp.where(keeps[half], lse, l2)
        o_ref[0, :, sl] = o2
        lse_ref[0, :, sl] = l2


def _merge_kernel(o1, o2, o3, l1, l2, l3, out):
    m = jnp.maximum(jnp.maximum(l1[...], l2[...]), l3[...])
    w1, w2, w3 = jnp.exp(l1[...] - m), jnp.exp(l2[...] - m), jnp.exp(l3[...] - m)
    ws = w1 + w2 + w3
    out[...] = (w1 / ws) * o1[...] + (w2 / ws) * o2[...] + (w3 / ws) * o3[...]


def _prompt_bias(rel_bias, n, dil):
    H = rel_bias.shape[1]
    m = 3 * n
    j = 2 * n - 1 - np.arange(m)
    u = _bias_by_distance(rel_bias, j * dil, (j >= 0) & (j <= n))
    rows = jnp.tile(u, (n, 1))[:n * (m - 1)].reshape(n, m - 1, H)
    return jnp.transpose(rows[:, n - 1:3 * n - 1], (2, 0, 1))


def _prompt_attn(q, k, v, rel_bias):
    B, T, HC = q.shape
    H = HC // HEAD_DIM
    outs, lses = [], []
    for window, dil in DILATIONS:
        n = window // dil
        assert T % window == 0 and HC % LANES == 0
        G = T // window
        view = lambda a: a.reshape(B, T // dil, dil * HC)
        cur = pl.BlockSpec((1, n, HC), lambda b, r, g: (b, g, r))
        prev = pl.BlockSpec((1, n, HC), lambda b, r, g: (b, jnp.maximum(g - 1, 0), r))
        bias = _prompt_bias(rel_bias, n, dil)
        o, lse = pl.pallas_call(
            _prompt_attn_kernel,
            grid=(B, dil, G),
            in_specs=[cur, prev, cur, prev, cur, pl.BlockSpec((H, n, 2 * n), lambda b, r, g: (0, 0, 0))],
            out_specs=[cur, cur],
            out_shape=[jax.ShapeDtypeStruct((B, T // dil, dil * HC), jnp.float32)] * 2,
            compiler_params=pltpu.CompilerParams(dimension_semantics=("parallel", "parallel", "arbitrary")),
            name="prompt_attn_d%d" % dil,
        )(view(q), view(k), view(k), view(v), view(v), bias)
        outs.append(o.reshape(B, T, HC))
        lses.append(lse.reshape(B, T, HC))
    tb = 512
    blk = pl.BlockSpec((1, tb, HC), lambda b, t: (b, t, 0))
    return pl.pallas_call(
        _merge_kernel, grid=(B, T // tb), in_specs=[blk] * 6, out_specs=blk,
        out_shape=jax.ShapeDtypeStruct((B, T, HC), jnp.float32),
        compiler_params=pltpu.CompilerParams(dimension_semantics=("parallel", "parallel")),
        name="prompt_attn_merge",
    )(*outs, *lses)


WKV_ROWS_PER_PASS = 4


def _wkv_kernel(w_ref, nkk_ref, b_ref, k_ref, r_ref, vop_ref, s0_ref, y_ref, s_ref, *, tt, vp, kd):
    j = pl.program_id(1)
    kgs = kd // SUBLANES
    G = WKV_ROWS_PER_PASS

    @pl.when(j == 0)
    def _():
        s_ref[...] = s0_ref[...]

    def step(t, c):
        for v0 in range(0, vp, G):
            accs = [None] * G
            for kg in range(kgs):
                ks = pl.ds(kg * SUBLANES, SUBLANES)
                nkk = nkk_ref[t, ks, :]
                for i in range(G):
                    p = s_ref[v0 + i, ks, :] * nkk
                    accs[i] = p if accs[i] is None else accs[i] + p
            sa = [jnp.sum(a, axis=0, keepdims=True) for a in accs]
            vv = [vop_ref[t, pl.ds(v0 + i, 1), :] for i in range(G)]
            yacc = [None] * G
            for kg in range(kgs):
                ks = pl.ds(kg * SUBLANES, SUBLANES)
                w = w_ref[t, ks, :]
                b = b_ref[t, ks, :]
                k = k_ref[t, ks, :]
                r = r_ref[t, ks, :]
                for i in range(G):
                    s2 = s_ref[v0 + i, ks, :] * w + sa[i] * b + vv[i] * k
                    s_ref[v0 + i, ks, :] = s2
                    p = s2 * r
                    yacc[i] = p if yacc[i] is None else yacc[i] + p
            for i in range(G):
                y_ref[t, pl.ds(v0 + i, 1), :] = jnp.sum(yacc[i], axis=0, keepdims=True)
        return c

    lax.fori_loop(0, tt, step, 0)


def _wkv_scan(r, w, k, v, kk, a, s0):
    B, T, H, N = r.shape
    bh = B * H
    nvh = max(1, LANES // bh)
    L = nvh * bh
    assert L % LANES == 0 and N % (nvh * WKV_ROWS_PER_PASS) == 0 and N % SUBLANES == 0
    vp = N // nvh
    tt = T if T <= 32 else 32
    assert T % tt == 0

    def kform(x):
        x = jnp.transpose(x, (1, 3, 0, 2)).reshape(T, N, 1, bh)
        return jnp.broadcast_to(x, (T, N, nvh, bh)).reshape(T, N, L)

    kops = [kform(w), kform(-kk), kform(kk * a), kform(k), kform(r)]
    vop = jnp.transpose(v, (1, 3, 0, 2)).reshape(T, nvh, vp, bh)
    vop = jnp.transpose(vop, (0, 2, 1, 3)).reshape(T, vp, L)
    s0t = jnp.transpose(s0, (2, 3, 0, 1)).reshape(nvh, vp, N, bh)
    s0t = jnp.transpose(s0t, (1, 2, 0, 3)).reshape(vp, N, L)
    kern = functools.partial(_wkv_kernel, tt=tt, vp=vp, kd=N)
    y, sf = pl.pallas_call(
        kern,
        grid=(L // LANES, T // tt),
        in_specs=[pl.BlockSpec((tt, N, LANES), lambda l, j: (j, 0, l))] * 5 + [
                  pl.BlockSpec((tt, vp, LANES), lambda l, j: (j, 0, l)),
                  pl.BlockSpec((vp, N, LANES), lambda l, j: (0, 0, l))],
        out_specs=[pl.BlockSpec((tt, vp, LANES), lambda l, j: (j, 0, l)),
                   pl.BlockSpec((vp, N, LANES), lambda l, j: (0, 0, l))],
        out_shape=[jax.ShapeDtypeStruct((T, vp, L), jnp.float32),
                   jax.ShapeDtypeStruct((vp, N, L), jnp.float32)],
        compiler_params=pltpu.CompilerParams(dimension_semantics=("parallel", "arbitrary"),
                                             vmem_limit_bytes=48 * 1024 * 1024),
        name="wkv_scan",
    )(*kops, vop, s0t)
    y = jnp.transpose(y.reshape(T, vp, nvh, B, H), (3, 0, 4, 2, 1)).reshape(B, T, H, N)
    sf = jnp.transpose(sf.reshape(vp, N, nvh, B, H), (3, 4, 2, 0, 1)).reshape(B, H, N, N)
    return y, sf


def _rwkv_mixer(pb, shift0, s0, mu, w0, w_w2, a0, w_a2, w_g2, k_k, k_a, r_k, lnx_g, lnx_b):
    B, T, _ = pb.shape
    pb = pb.astype(jnp.float32)
    prev = jnp.concatenate([shift0[:, None, :].astype(jnp.float32), pb[:, :-1]], axis=1)
    xm = pb + (prev - pb) * mu
    c = C_RWKV
    r, k, v = xm[..., :c], xm[..., c:2 * c], xm[..., 2 * c:3 * c]
    o = 3 * c
    wl = xm[..., o:o + LORA_DECAY]
    al = xm[..., o + LORA_DECAY:o + LORA_DECAY + LORA_ICLR]
    gl = xm[..., o + LORA_DECAY + LORA_ICLR:]
    w_log = -jax.nn.softplus(-(w0 + jnp.tanh(wl) @ w_w2)) - 0.5
    decay = jnp.exp(-jnp.exp(w_log))
    a = jax.nn.sigmoid(a0 + al @ w_a2)
    g = jax.nn.sigmoid(gl) @ w_g2

    def heads(t):
        return t.reshape(B, T, N_HEADS_RWKV, HEAD_DIM)
    kk = heads(k * k_k)
    kk = kk / jnp.maximum(jnp.sqrt(jnp.sum(kk * kk, axis=-1, keepdims=True)), 1e-12)
    k = k * (1.0 + (a - 1.0) * k_a)
    r_h, k_h, v_h = heads(r), heads(k), heads(v)
    y, s_fin = _wkv_scan(r_h, heads(decay), k_h, v_h, kk, heads(a), s0.astype(jnp.float32))
    mean = jnp.mean(y, axis=-1, keepdims=True)
    var = jnp.mean(jnp.square(y - mean), axis=-1, keepdims=True)
    y = ((y - mean) * lax.rsqrt(var + GN_EPS)).reshape(B, T, C_RWKV) * lnx_g + lnx_b
    bonus = jnp.sum(r_h * k_h * r_k, axis=-1, keepdims=True) * v_h
    y = (y + bonus.reshape(B, T, C_RWKV)) * g
    return y, s_fin, pb[:, -1]


def _gelu_exact(x):
    return 0.5 * x * (1.0 + lax.erf(x * (2.0 ** -0.5)))


PEER_TOK = 8


def _peer_expert_kernel(eidc_ref, eidn_ref, h_ref, gate_ref, tab_ref, out_ref, buf, sem, *, rows, dim):
    i = pl.program_id(0)
    n = pl.num_programs(0)
    tok = PEER_TOK
    chunks = dim // LANES
    erow = chunks
    groups = rows // SUBLANES
    rs = erow + 1

    def start_row_copy(eid_ref, src_t, t, r, slot):
        e = eid_ref[src_t, r]
        src = tab_ref.at[pl.ds(pl.multiple_of(e * erow, erow), erow)]
        dst = buf.at[pl.ds(((slot * tok + t) * rows + r) * rs, erow)]
        pltpu.make_async_copy(src, dst, sem.at[slot]).start(priority=r % 2)

    def wait_slot(slot):
        region = buf.at[pl.ds(0, tok * rows * erow)]
        pltpu.make_async_copy(region, region, sem.at[slot]).wait()

    def compute(t_blk, t, slot):
        r0 = (slot * tok + t) * rows * rs

        def words(g, k):
            return buf[pl.ds(r0 + g * SUBLANES * rs + k, SUBLANES, stride=rs), :]
        hb = [jnp.broadcast_to(h_ref[t_blk:t_blk + 1, k * LANES:(k + 1) * LANES], (SUBLANES, LANES))
              for k in range(chunks)]
        lane = lax.broadcasted_iota(jnp.int32, (SUBLANES, LANES), 1)
        hi_mask = jnp.uint32(0xFFFF0000)
        s_tile = jnp.zeros((SUBLANES, LANES), jnp.float32)
        for g in range(groups):
            acc = None
            for k in range(chunks):
                p = pltpu.bitcast(words(g, k) & hi_mask, jnp.float32) * hb[k]
                acc = p if acc is None else acc + p
            s_tile = jnp.where(lane == g, jnp.sum(acc, axis=1, keepdims=True), s_tile)
        w_tile = _gelu_exact(s_tile) * gate_ref[t_blk]
        accs = [None] * chunks
        for g in range(groups):
            wg = jnp.broadcast_to(w_tile[:, g:g + 1], (SUBLANES, LANES))
            for k in range(chunks):
                p = wg * pltpu.bitcast(words(g, k) << 16, jnp.float32)
                accs[k] = p if accs[k] is None else accs[k] + p
        out_ref[t_blk:t_blk + 1, :] = jnp.concatenate([jnp.sum(a, axis=0, keepdims=True) for a in accs], axis=1)

    @pl.when(i == 0)
    def _():
        for t in range(tok):
            for r in range(rows):
                start_row_copy(eidc_ref, t, t, r, 0)

    wait_slot(0)
    for t in range(tok):
        for r in range(rows):
            start_row_copy(eidc_ref, tok + t, t, r, 1)
        compute(t, t, 0)
    wait_slot(1)
    for t in range(tok):
        for r in range(rows):
            start_row_copy(eidn_ref, t, t, r, 0)
        compute(tok + t, t, 1)

    @pl.when(i == n - 1)
    def _():
        wait_slot(0)


def peer_experts(h, eid, gate_tile, expert_u, expert_v):
    N, D = h.shape
    R = eid.shape[1]
    E = expert_u.shape[0]
    chunks = D // LANES
    step_tok = 2 * PEER_TOK
    assert N % step_tok == 0 and R % SUBLANES == 0 and D % LANES == 0
    groups = R // SUBLANES

    def bf16_bits(x):
        return lax.bitcast_convert_type(x.astype(jnp.bfloat16), jnp.uint16).astype(jnp.uint32)
    tab = ((bf16_bits(expert_u) << 16) | bf16_bits(expert_v)).reshape(E * chunks, LANES)
    nsteps = N // step_tok
    kern = functools.partial(_peer_expert_kernel, rows=R, dim=D)
    return pl.pallas_call(
        kern,
        grid=(nsteps,),
        in_specs=[
            pl.BlockSpec((step_tok, R), lambda i: (i, 0), memory_space=pltpu.SMEM),
            pl.BlockSpec((step_tok, R), lambda i: (jnp.minimum(i + 1, nsteps - 1), 0), memory_space=pltpu.SMEM),
            pl.BlockSpec((step_tok, D), lambda i: (i, 0)),
            pl.BlockSpec((step_tok, SUBLANES, LANES), lambda i: (i, 0, 0)),
            pl.BlockSpec(memory_space=pl.ANY),
        ],
        out_specs=pl.BlockSpec((step_tok, D), lambda i: (i, 0)),
        out_shape=jax.ShapeDtypeStruct((N, D), jnp.float32),
        scratch_shapes=[pltpu.VMEM((2 * PEER_TOK * R * (chunks + 1), LANES), jnp.uint32),
                        pltpu.SemaphoreType.DMA((2,))],
        compiler_params=pltpu.CompilerParams(dimension_semantics=("arbitrary",),
                                             vmem_limit_bytes=48 * 1024 * 1024,
                                             disable_bounds_checks=True),
        name="peer_experts",
    )(eid, eid, h, gate_tile, tab)


PEER_PAIRS = tuple((a, b) for a in range(PEER_TOPK) for b in range(PEER_TOPK) if (a + 1) * (b + 1) <= PEER_TOPK)


def _top_rows(x, pos, k):
    vals, idxs = [], []
    for _ in range(k):
        m = jnp.max(x, axis=0, keepdims=True)
        i = jnp.min(jnp.where(x == m, pos, jnp.float32(1e9)), axis=0, keepdims=True)
        vals.append(m)
        idxs.append(i)
        x = jnp.where(pos == i, -jnp.inf, x)
    return vals, idxs


def _joint_topk(s_ref, eid_ref, gate_ref, *, heads, keys, topk):
    tb = s_ref.shape[1]
    key_pos = lax.broadcasted_iota(jnp.int32, (keys, tb), 0).astype(jnp.float32)
    nrow = -(-len(PEER_PAIRS) // SUBLANES) * SUBLANES
    pair_pos = lax.broadcasted_iota(jnp.int32, (nrow, tb), 0).astype(jnp.float32)
    for h in range(heads):
        v1, i1 = _top_rows(s_ref[pl.ds((2 * h) * keys, keys), :], key_pos, topk)
        v2, i2 = _top_rows(s_ref[pl.ds((2 * h + 1) * keys, keys), :], key_pos, topk)
        cand = jnp.full((nrow, tb), -jnp.inf, jnp.float32)
        cid = jnp.zeros((nrow, tb), jnp.float32)
        for p, (a, b) in enumerate(PEER_PAIRS):
            cand = jnp.where(pair_pos == p, v1[a] + v2[b], cand)
            cid = jnp.where(pair_pos == p, i1[a] * keys + i2[b], cid)
        top_s, top_p = _top_rows(cand, pair_pos, topk)
        es = [jnp.exp(s - top_s[0]) for s in top_s]
        den = es[0]
        for e in es[1:]:
            den = den + e
        for r in range(topk):
            eid = jnp.sum(jnp.where(pair_pos == top_p[r], cid, 0.0), axis=0, keepdims=True)
            eid_ref[pl.ds(h * topk + r, 1), :] = eid.astype(jnp.int32)
            gate_ref[pl.ds(h * topk + r, 1), :] = es[r] / den


def _route_kernel(h_ref, wq_ref, sk_ref, eid_ref, gate_ref, s_scr, *, heads, keys, topk):
    q = jnp.dot(h_ref[...].astype(jnp.bfloat16), wq_ref[...], preferred_element_type=jnp.float32)
    half = sk_ref.shape[2]
    nt = (((1,), (1,)), ((), ()))
    for g in range(2 * heads):
        qg = q[:, g * half:(g + 1) * half].astype(jnp.bfloat16)
        s_scr[pl.ds(g * keys, keys), :] = lax.dot_general(sk_ref[g], qg, nt, preferred_element_type=jnp.float32)
    _joint_topk(s_scr, eid_ref, gate_ref, heads=heads, keys=keys, topk=topk)


def _peer_route(h, w_pq, sub_keys, *, tb=LANES):
    n_tok, D = h.shape
    heads, _, keys, half = sub_keys.shape
    assert n_tok % tb == 0
    wq = w_pq.astype(jnp.bfloat16)
    sk = sub_keys.reshape(heads * 2, keys, half).astype(jnp.bfloat16)
    kern = functools.partial(_route_kernel, heads=heads, keys=keys, topk=PEER_TOPK)
    const = lambda a: pl.BlockSpec(a.shape, lambda i: (0,) * a.ndim)
    eid_t, gate_t = pl.pallas_call(
        kern,
        grid=(n_tok // tb,),
        in_specs=[pl.BlockSpec((tb, D), lambda i: (i, 0)), const(wq), const(sk)],
        out_specs=[pl.BlockSpec((heads * PEER_TOPK, tb), lambda i: (0, i)),
                   pl.BlockSpec((heads * PEER_TOPK, tb), lambda i: (0, i))],
        out_shape=[jax.ShapeDtypeStruct((heads * PEER_TOPK, n_tok), jnp.int32),
                   jax.ShapeDtypeStruct((heads * PEER_TOPK, n_tok), jnp.float32)],
        scratch_shapes=[pltpu.VMEM((heads * 2 * keys, tb), jnp.float32)],
        compiler_params=pltpu.CompilerParams(dimension_semantics=("parallel",),
                                             vmem_limit_bytes=48 * 1024 * 1024),
        name="peer_route",
    )(h, wq, sk)
    groups = PEER_HEADS * PEER_TOPK // SUBLANES
    gate_tile = jnp.transpose(gate_t.reshape(groups, SUBLANES, n_tok), (2, 1, 0))
    gate_tile = jnp.pad(gate_tile, ((0, 0), (0, 0), (0, LANES - groups)))
    return eid_t.T, gate_tile


def _mix_proj_kernel(oa_ref, yr_ref, x_ref, g1_ref, sc_ref, sh_ref, gn_ref, wa_ref, wr_ref, x1_ref, h2_ref):
    mix = (jnp.dot(oa_ref[0].astype(jnp.bfloat16), wa_ref[...], preferred_element_type=jnp.float32)
           + jnp.dot(yr_ref[0].astype(jnp.bfloat16), wr_ref[...], preferred_element_type=jnp.float32))
    x1 = x_ref[0] + g1_ref[0] * mix
    x1_ref[0] = x1
    h = x1 * lax.rsqrt(jnp.mean(x1 * x1, axis=-1, keepdims=True) + NORM_EPS) * gn_ref[...]
    h2_ref[0] = h * (1.0 + sc_ref[0]) + sh_ref[0]


def _mix_proj(o_attn, y_rwkv, x, g1, sc2, sh2, gain2, w_out):
    B, T, D = x.shape
    ca, cr = o_attn.shape[2], y_rwkv.shape[2]
    tm = min(T, 512)
    assert T % tm == 0
    blk = lambda w: pl.BlockSpec((1, tm, w), lambda b, t: (b, t, 0))
    mod = _mod_spec(g1, tm)
    const = lambda a: pl.BlockSpec(a.shape, lambda b, t: (0,) * a.ndim)
    wa, wr = w_out[:ca].astype(jnp.bfloat16), w_out[ca:].astype(jnp.bfloat16)
    gn = gain2.reshape(1, D)
    return pl.pallas_call(
        _mix_proj_kernel,
        grid=(B, T // tm),
        in_specs=[blk(ca), blk(cr), blk(D), mod, mod, mod, const(gn), const(wa), const(wr)],
        out_specs=[blk(D), blk(D)],
        out_shape=[jax.ShapeDtypeStruct((B, T, D), jnp.float32)] * 2,
        compiler_params=pltpu.CompilerParams(dimension_semantics=("parallel", "parallel"),
                                             vmem_limit_bytes=48 * 1024 * 1024),
        name="mix_proj",
    )(o_attn, y_rwkv, x, g1, sc2, sh2, gn, wa, wr)


def _resid_kernel(x_ref, g_ref, f_ref, o_ref):
    o_ref[...] = x_ref[...] + g_ref[...] * f_ref[...]


def _resid(x, g, f):
    B, T, D = x.shape
    tb = min(T, 512)
    return pl.pallas_call(
        _resid_kernel,
        grid=(B, T // tb),
        in_specs=[pl.BlockSpec((1, tb, D), lambda b, t: (b, t, 0)),
                  pl.BlockSpec((1, 1, D), lambda b, t: (b, 0, 0)),
                  pl.BlockSpec((1, tb, D), lambda b, t: (b, t, 0))],
        out_specs=pl.BlockSpec((1, tb, D), lambda b, t: (b, t, 0)),
        out_shape=jax.ShapeDtypeStruct(x.shape, x.dtype),
        name="ffn_residual",
    )(x, g, f)


def _layer(x, c, k_buf, v_buf, wkv0, shift0, rel_bias, p):
    B, T, _ = x.shape
    mod = jax.nn.silu(c.astype(jnp.float32)) @ p['ada_w'] + p['ada_b']
    sh1, sc1, g1, sh2, sc2, g2 = jnp.split(mod[:, None, :], 6, axis=-1)
    if T < SUBLANES:
        fold = lambda a: a.reshape(1, B * T, a.shape[-1])
        per_token = lambda m: fold(jnp.broadcast_to(m, (B, T, D_MODEL)))
    else:
        fold = per_token = lambda a: a
    unfold = lambda a: a.reshape(B, T, a.shape[-1])
    q3, k3, v3, proj_rwkv = (unfold(a) for a in _in_proj(
        fold(x), p['norm1_g'], per_token(sc1), per_token(sh1), p['w_in'], p['q_norm_g'], p['k_norm_g']))
    q, k, v = (a.reshape(B, T, N_HEADS_ATTN, HEAD_DIM) for a in (q3, k3, v3))
    if k_buf is None:
        o_attn = _prompt_attn(q3, k3, v3, rel_bias)
        keep = min(MAX_WINDOW, T)
        k_rows, v_rows = k[:, T - keep:], v[:, T - keep:]
        wkv0 = jnp.zeros((B, N_HEADS_RWKV, HEAD_DIM, HEAD_DIM), jnp.float32)
        shift0 = jnp.zeros((B, COLS_RWKV), jnp.float32)
    else:
        o_attn = _sample_attn(q, k, v, jnp.transpose(k_buf, (0, 2, 3, 1)), jnp.transpose(v_buf, (0, 2, 3, 1)), rel_bias)
        k_rows, v_rows = k, v
    y_rwkv, s_fin, shift_new = _rwkv_mixer(
        proj_rwkv, shift0, wkv0, p['mu_shift'], p['w0'], p['w_w2'], p['a0'], p['w_a2'],
        p['w_g2'], p['k_k'], p['k_a'], p['r_k'], p['lnx_g'], p['lnx_b'])
    x, h2 = (unfold(a) for a in _mix_proj(
        fold(o_attn.reshape(B, T, C_ATTN)), fold(y_rwkv), fold(x), per_token(g1), per_token(sc2), per_token(sh2),
        p['norm2_g'], p['w_out']))
    return x, h2, g2, k_rows, v_rows, s_fin, shift_new


def kernel(x_prompt, x_sample, c_prompt, c_sample, cache_k_win, cache_v_win, state_wkv, state_shift,
           ada_w, ada_b, norm1_g, norm2_g, w_in, q_norm_g, k_norm_g, rel_bias, mu_shift, w0, w_w2, a0,
           w_a2, w_g2, k_k, k_a, r_k, lnx_g, lnx_b, w_out, w_peer_q, peer_sub_keys, expert_u, expert_v):
    xp, xs = x_prompt, x_sample
    kp_l, vp_l, sp_l, hp_l = [], [], [], []
    ks_l, vs_l, ss_l, hs_l = [], [], [], []
    names = ('ada_w', 'ada_b', 'norm1_g', 'norm2_g', 'w_in', 'q_norm_g', 'k_norm_g', 'mu_shift', 'w0', 'w_w2',
             'a0', 'w_a2', 'w_g2', 'k_k', 'k_a', 'r_k', 'lnx_g', 'lnx_b', 'w_out', 'w_peer_q', 'peer_sub_keys',
             'expert_u', 'expert_v')
    vals = (ada_w, ada_b, norm1_g, norm2_g, w_in, q_norm_g, k_norm_g, mu_shift, w0, w_w2, a0, w_a2, w_g2, k_k,
            k_a, r_k, lnx_g, lnx_b, w_out, w_peer_q, peer_sub_keys, expert_u, expert_v)
    for l in range(DEPTH):
        p = {n: v[l] for n, v in zip(names, vals)}
        xp, h2p, g2p, kp, vp, sp, hp = _layer(xp, c_prompt, None, None, None, None, rel_bias, p)
        xs, h2s, g2s, kn, vn, sn, hn = _layer(xs, c_sample, cache_k_win[l], cache_v_win[l], state_wkv[l],
                                              state_shift[l], rel_bias, p)
        n_p = h2p.shape[0] * h2p.shape[1]
        h2 = jnp.concatenate([h2p.reshape(-1, D_MODEL), h2s.reshape(-1, D_MODEL)], axis=0)
        eid, gate_tile = _peer_route(h2, p['w_peer_q'], p['peer_sub_keys'])
        ffn = peer_experts(h2, eid, gate_tile, p['expert_u'], p['expert_v'])
        xp = _resid(xp, g2p, ffn[:n_p].reshape(xp.shape))
        xs = _resid(xs, g2s, ffn[n_p:].reshape(xs.shape))
        kp_l.append(kp); vp_l.append(vp); sp_l.append(sp); hp_l.append(hp)
        ks_l.append(kn); vs_l.append(vn); ss_l.append(sn); hs_l.append(hn)
    return (xp, xs, jnp.stack(kp_l), jnp.stack(vp_l), jnp.stack(sp_l), jnp.stack(hp_l),
            jnp.stack(ks_l), jnp.stack(vs_l), jnp.stack(ss_l), jnp.stack(hs_l))
```

```python
import functools
import math
import jax, jax.numpy as jnp
from jax import lax
import numpy as np
from jax.experimental import pallas as pl
from jax.experimental.pallas import tpu as pltpu

D_MODEL = 1024
DEPTH = 1
HEAD_DIM = 64
N_HEADS_ATTN = 8
N_HEADS_RWKV = 8
C_ATTN = N_HEADS_ATTN * HEAD_DIM
C_RWKV = N_HEADS_RWKV * HEAD_DIM
DILATIONS = ((128, 1), (512, 4), (2048, 16))
MAX_WINDOW = 2048
N_BUCKETS = 32
MAX_DISTANCE = 2048
LORA_DECAY = 32
LORA_ICLR = 32
LORA_GATE = 64
COLS_RWKV = 3 * C_RWKV + LORA_DECAY + LORA_ICLR + LORA_GATE
D_IN = 3 * C_ATTN + COLS_RWKV
PEER_HEADS = 8
PEER_KEYS = 128
PEER_TOPK = 16
NORM_EPS = 1e-6
GN_EPS = 64e-5
NEG_INF = -1e30
ATTN_SCALE = HEAD_DIM ** -0.5


def _t5_bucket(dist):
    dist = np.asarray(dist, dtype=np.int64)
    max_exact = N_BUCKETS // 2
    safe = np.maximum(dist, 1) / max_exact
    large = max_exact + (np.log(safe) / math.log(MAX_DISTANCE / max_exact) * (N_BUCKETS - max_exact)).astype(np.int64)
    large = np.minimum(large, N_BUCKETS - 1)
    return np.where(dist < max_exact, dist, large).astype(np.int32)


LANES = 128
SUBLANES = 8


def _mod_spec(a, tm):
    per_token = a.shape[1] != 1
    return pl.BlockSpec((1, tm if per_token else 1, a.shape[2]), lambda b, t: (b, t if per_token else 0, 0))


def _segment_mean_matrix(width, seg):
    lane = np.arange(width)
    return jnp.asarray((lane[:, None] // seg == lane[None, :] // seg).astype(np.float32) / seg, jnp.bfloat16)


def _head_rms(x, seg_ref, gain):
    x2 = x * x
    hi = x2.astype(jnp.bfloat16)
    lo = (x2 - hi.astype(jnp.float32)).astype(jnp.bfloat16)
    ms = (jnp.dot(hi, seg_ref[...], preferred_element_type=jnp.float32)
          + jnp.dot(lo, seg_ref[...], preferred_element_type=jnp.float32))
    return x * lax.rsqrt(ms + NORM_EPS) * gain


def _in_proj_kernel(x_ref, g_ref, sc_ref, sh_ref, w_ref, seg_ref, gq_ref, gk_ref, q_ref, k_ref, v_ref, r_ref):
    x = x_ref[0]
    h = x * lax.rsqrt(jnp.mean(x * x, axis=-1, keepdims=True) + NORM_EPS) * g_ref[...]
    h = (h * (1.0 + sc_ref[0]) + sh_ref[0]).astype(jnp.bfloat16)
    c = q_ref.shape[2]
    q_ref[0] = _head_rms(jnp.dot(h, w_ref[:, :c], preferred_element_type=jnp.float32), seg_ref, gq_ref[...])
    k_ref[0] = _head_rms(jnp.dot(h, w_ref[:, c:2 * c], preferred_element_type=jnp.float32), seg_ref, gk_ref[...])
    v_ref[0] = jnp.dot(h, w_ref[:, 2 * c:3 * c], preferred_element_type=jnp.float32)
    r_ref[0] = jnp.dot(h, w_ref[:, 3 * c:], preferred_element_type=jnp.float32)


def _in_proj(x, gain, sc, sh, w_in, q_gain, k_gain):
    B, T, D = x.shape
    n_out = w_in.shape[1]
    c = C_ATTN
    tm = min(T, 512)
    assert T % tm == 0
    heads = c // HEAD_DIM
    row = lambda a: a.reshape(1, -1)
    blk = lambda w: pl.BlockSpec((1, tm, w), lambda b, t: (b, t, 0))
    const = lambda a: pl.BlockSpec(a.shape, lambda b, t: (0,) * a.ndim)
    mod = _mod_spec(sc, tm)
    seg = _segment_mean_matrix(c, HEAD_DIM)
    wb = w_in.astype(jnp.bfloat16)
    gq, gk = row(jnp.tile(q_gain, heads)), row(jnp.tile(k_gain, heads))
    return pl.pallas_call(
        _in_proj_kernel,
        grid=(B, T // tm),
        in_specs=[blk(D), const(row(gain)), mod, mod, const(wb), const(seg), const(gq), const(gk)],
        out_specs=[blk(c), blk(c), blk(c), blk(n_out - 3 * c)],
        out_shape=[jax.ShapeDtypeStruct((B, T, c), jnp.float32)] * 3
                  + [jax.ShapeDtypeStruct((B, T, n_out - 3 * c), jnp.float32)],
        compiler_params=pltpu.CompilerParams(dimension_semantics=("parallel", "parallel"),
                                             vmem_limit_bytes=56 * 1024 * 1024),
        name="in_proj",
    )(x, row(gain), sc, sh, wb, seg, gq, gk)


def _sample_attn_kernel(q_ref, kn_ref, vn_ref, k_ref, v_ref, b1_ref, b2_ref, b3_ref, bn_ref, o_ref,
                        d_scr, dn_scr, p_scr, pn_scr):
    _, H, S, C = q_ref.shape
    P = k_ref.shape[3]
    for h in range(H):
        rows = pl.ds(h * S, S)
        d_scr[rows, :] = jnp.dot(q_ref[0, h], k_ref[0, h], preferred_element_type=jnp.float32) * ATTN_SCALE
        dn_scr[rows, :] = jnp.dot(q_ref[0, h], kn_ref[0, h], preferred_element_type=jnp.float32) * ATTN_SCALE
    d, dn = d_scr[...], dn_scr[...]
    ecs, ens, dens, lses = [], [], [], []
    for i, b_ref in enumerate((b1_ref, b2_ref, b3_ref)):
        w = b_ref.shape[1]
        lc = d[:, P - w:] + b_ref[...]
        ln = dn + bn_ref[i]
        m = jnp.maximum(jnp.max(lc, axis=1, keepdims=True), jnp.max(ln, axis=1, keepdims=True))
        ecs.append(jnp.exp(lc - m))
        ens.append(jnp.exp(ln - m))
        dens.append(jnp.sum(ecs[i], axis=1, keepdims=True) + jnp.sum(ens[i], axis=1, keepdims=True))
        lses.append(m + jnp.log(dens[i]))
    mm = jnp.maximum(jnp.maximum(lses[0], lses[1]), lses[2])
    ws = [jnp.exp(l - mm) for l in lses]
    wsum = ws[0] + ws[1] + ws[2]
    coef = [ws[i] / (wsum * dens[i]) for i in range(3)]
    w1, w2 = b1_ref.shape[1], b2_ref.shape[1]
    p3 = ecs[2] * coef[2]
    p2 = ecs[1] * coef[1]
    p_scr[:, :P - w2] = p3[:, :P - w2]
    p_scr[:, P - w2:P - w1] = p3[:, P - w2:P - w1] + p2[:, :w2 - w1]
    p_scr[:, P - w1:] = p3[:, P - w1:] + p2[:, w2 - w1:] + ecs[0] * coef[0]
    pn_scr[...] = ens[0] * coef[0] + ens[1] * coef[1] + ens[2] * coef[2]
    lane = lax.broadcasted_iota(jnp.int32, (C, S), 1)
    for h in range(H):
        o_tile = jnp.zeros((C, S), jnp.float32)
        for s in range(S):
            r = h * S + s
            acc = None
            for j in range(P // LANES):
                t = v_ref[0, h, :, j * LANES:(j + 1) * LANES] * p_scr[r:r + 1, j * LANES:(j + 1) * LANES]
                acc = t if acc is None else acc + t
            col = jnp.sum(acc, axis=1, keepdims=True) + jnp.sum(vn_ref[0, h] * pn_scr[r:r + 1, :], axis=1, keepdims=True)
            o_tile = jnp.where(lane == s, col, o_tile)
        o_ref[0, h] = o_tile


def _bias_by_distance(rel_bias, dist, valid):
    onehot = np.eye(N_BUCKETS, dtype=np.float32)[_t5_bucket(np.maximum(dist, 0))]
    b = jnp.dot(onehot, rel_bias.astype(jnp.float32), precision=lax.Precision.HIGHEST)
    return jnp.where(valid[:, None], b, NEG_INF)


def _sample_bias_tables(rel_bias, S, lb):
    tabs, news = [], []
    H = rel_bias.shape[1]
    for window, dil in DILATIONS:
        d = np.arange(window + S + 1)
        vec = _bias_by_distance(rel_bias, d, (d % dil == 0) & (d >= 1) & (d <= window))
        rows = [vec[s + 1:s + 1 + window][::-1] for s in range(S)]
        tabs.append(jnp.transpose(jnp.stack(rows), (2, 0, 1)).reshape(H * S, window))
        dn = (np.arange(S)[:, None] - np.arange(S)[None, :]).reshape(-1)
        bn = _bias_by_distance(rel_bias, dn, (dn >= 0) & (dn % dil == 0) & (dn <= window)).reshape(S, S, H)
        news.append(jnp.transpose(bn, (2, 0, 1)).reshape(H * S, S))
    return tabs, jnp.stack(news)


def _sample_attn(q, k_new, v_new, k_buf_t, v_buf_t, rel_bias):
    B, S, H, C = q.shape
    lb = k_buf_t.shape[3]
    assert lb >= DILATIONS[-1][0] and all(w % LANES == 0 for w, _ in DILATIONS)
    tabs, bn = _sample_bias_tables(rel_bias, S, lb)
    tr = lambda a: jnp.transpose(a, (0, 2, 3, 1))
    qs = pl.BlockSpec((1, H, S, C), lambda b: (b, 0, 0, 0))
    tok = pl.BlockSpec((1, H, C, S), lambda b: (b, 0, 0, 0))
    cache = pl.BlockSpec((1, H, C, lb), lambda b: (b, 0, 0, 0))
    full = lambda a: pl.BlockSpec(a.shape, lambda b: (0,) * a.ndim)
    o = pl.pallas_call(
        _sample_attn_kernel,
        grid=(B,),
        in_specs=[qs, tok, tok, cache, cache, full(tabs[0]), full(tabs[1]), full(tabs[2]), full(bn)],
        out_specs=tok,
        out_shape=jax.ShapeDtypeStruct((B, H, C, S), jnp.float32),
        scratch_shapes=[pltpu.VMEM((H * S, lb), jnp.float32), pltpu.VMEM((H * S, S), jnp.float32),
                        pltpu.VMEM((H * S, lb), jnp.float32), pltpu.VMEM((H * S, S), jnp.float32)],
        compiler_params=pltpu.CompilerParams(dimension_semantics=("parallel",),
                                             vmem_limit_bytes=48 * 1024 * 1024),
        name="sample_attn",
    )(jnp.transpose(q, (0, 2, 1, 3)), tr(k_new), tr(v_new), k_buf_t, v_buf_t, *tabs, bn)
    return jnp.transpose(o, (0, 3, 1, 2))


def _prompt_attn_kernel(q_ref, kp_ref, kc_ref, vp_ref, vc_ref, bias_ref, o_ref, lse_ref):
    g = pl.program_id(2)
    n = q_ref.shape[1]
    heads = bias_ref.shape[0]
    lane = lax.broadcasted_iota(jnp.int32, (n, LANES), 1)
    nt = (((1,), (1,)), ((), ()))
    per = LANES // HEAD_DIM
    sls = [slice(hp * LANES, (hp + 1) * LANES) for hp in range(heads // per)]
    keeps = [(lane >= half * HEAD_DIM) & (lane < (half + 1) * HEAD_DIM) for half in range(per)]
    logits = []
    for h in range(heads):
        sl, keep = sls[h // per], keeps[h % per]
        qh = jnp.where(keep, q_ref[0, :, sl], 0.0)
        lp = lax.dot_general(qh, kp_ref[0, :, sl], nt, preferred_element_type=jnp.float32) * ATTN_SCALE + bias_ref[h, :, :n]
        lc = lax.dot_general(qh, kc_ref[0, :, sl], nt, preferred_element_type=jnp.float32) * ATTN_SCALE + bias_ref[h, :, n:]
        logits.append((jnp.where(g == 0, NEG_INF, lp), lc))
    probs = []
    for lp, lc in logits:
        m = jnp.max(jnp.maximum(lp, lc), axis=-1, keepdims=True)
        ep = jnp.exp(lp - m)
        ec = jnp.exp(lc - m)
        s = jnp.sum(ep + ec, axis=-1, keepdims=True)
        probs.append((ep, ec, s, m + jnp.log(s)))
    for hp, sl in enumerate(sls):
        o2 = jnp.zeros((n, LANES), jnp.float32)
        l2 = jnp.zeros((n, LANES), jnp.float32)
        for half in range(per):
            ep, ec, s, lse = probs[hp * per + half]
            o = (jnp.dot(ep, vp_ref[0, :, sl], preferred_element_type=jnp.float32)
                 + jnp.dot(ec, vc_ref[0, :, sl], preferred_element_type=jnp.float32)) / s
            o2 = jnp.where(keeps[half], o, o2)
            l2 = jnp.where(keeps[half], lse, l2)
        o_ref[0, :, sl] = o2
        lse_ref[0, :, sl] = l2


def _merge_kernel(o1, o2, o3, l1, l2, l3, out):
    m = jnp.maximum(jnp.maximum(l1[...], l2[...]), l3[...])
    w1, w2, w3 = jnp.exp(l1[...] - m), jnp.exp(l2[...] - m), jnp.exp(l3[...] - m)
    ws = w1 + w2 + w3
    out[...] = (w1 / ws) * o1[...] + (w2 / ws) * o2[...] + (w3 / ws) * o3[...]


def _prompt_bias(rel_bias, n, dil):
    H = rel_bias.shape[1]
    m = 3 * n
    j = 2 * n - 1 - np.arange(m)
    u = _bias_by_distance(rel_bias, j * dil, (j >= 0) & (j <= n))
    rows = jnp.tile(u, (n, 1))[:n * (m - 1)].reshape(n, m - 1, H)
    return jnp.transpose(rows[:, n - 1:3 * n - 1], (2, 0, 1))


def _prompt_attn(q, k, v, rel_bias):
    B, T, HC = q.shape
    H = HC // HEAD_DIM
    outs, lses = [], []
    for window, dil in DILATIONS:
        n = window // dil
        assert T % window == 0 and HC % LANES == 0
        G = T // window
        view = lambda a: a.reshape(B, T // dil, dil * HC)
        cur = pl.BlockSpec((1, n, HC), lambda b, r, g: (b, g, r))
        prev = pl.BlockSpec((1, n, HC), lambda b, r, g: (b, jnp.maximum(g - 1, 0), r))
        bias = _prompt_bias(rel_bias, n, dil)
        o, lse = pl.pallas_call(
            _prompt_attn_kernel,
            grid=(B, dil, G),
            in_specs=[cur, prev, cur, prev, cur, pl.BlockSpec((H, n, 2 * n), lambda b, r, g: (0, 0, 0))],
            out_specs=[cur, cur],
            out_shape=[jax.ShapeDtypeStruct((B, T // dil, dil * HC), jnp.float32)] * 2,
            compiler_params=pltpu.CompilerParams(dimension_semantics=("parallel", "parallel", "arbitrary")),
            name="prompt_attn_d%d" % dil,
        )(view(q), view(k), view(k), view(v), view(v), bias)
        outs.append(o.reshape(B, T, HC))
        lses.append(lse.reshape(B, T, HC))
    tb = 512
    blk = pl.BlockSpec((1, tb, HC), lambda b, t: (b, t, 0))
    return pl.pallas_call(
        _merge_kernel, grid=(B, T // tb), in_specs=[blk] * 6, out_specs=blk,
        out_shape=jax.ShapeDtypeStruct((B, T, HC), jnp.float32),
        compiler_params=pltpu.CompilerParams(dimension_semantics=("parallel", "parallel")),
        name="prompt_attn_merge",
    )(*outs, *lses)


WKV_ROWS_PER_PASS = 4


def _wkv_kernel(w_ref, nkk_ref, b_ref, k_ref, r_ref, vop_ref, s0_ref, y_ref, s_ref, *, tt, vp, kd):
    j = pl.program_id(1)
    kgs = kd // SUBLANES
    G = WKV_ROWS_PER_PASS

    @pl.when(j == 0)
    def _():
        s_ref[...] = s0_ref[...]

    def step(t, c):
        for v0 in range(0, vp, G):
            accs = [None] * G
            for kg in range(kgs):
                ks = pl.ds(kg * SUBLANES, SUBLANES)
                nkk = nkk_ref[t, ks, :]
                for i in range(G):
                    p = s_ref[v0 + i, ks, :] * nkk
                    accs[i] = p if accs[i] is None else accs[i] + p
            sa = [jnp.sum(a, axis=0, keepdims=True) for a in accs]
            vv = [vop_ref[t, pl.ds(v0 + i, 1), :] for i in range(G)]
            yacc = [None] * G
            for kg in range(kgs):
                ks = pl.ds(kg * SUBLANES, SUBLANES)
                w = w_ref[t, ks, :]
                b = b_ref[t, ks, :]
                k = k_ref[t, ks, :]
                r = r_ref[t, ks, :]
                for i in range(G):
                    s2 = s_ref[v0 + i, ks, :] * w + sa[i] * b + vv[i] * k
                    s_ref[v0 + i, ks, :] = s2
                    p = s2 * r
                    yacc[i] = p if yacc[i] is None else yacc[i] + p
            for i in range(G):
                y_ref[t, pl.ds(v0 + i, 1), :] = jnp.sum(yacc[i], axis=0, keepdims=True)
        return c

    lax.fori_loop(0, tt, step, 0)


def _wkv_scan(r, w, k, v, kk, a, s0):
    B, T, H, N = r.shape
    bh = B * H
    nvh = max(1, LANES // bh)
    L = nvh * bh
    assert L % LANES == 0 and N % (nvh * WKV_ROWS_PER_PASS) == 0 and N % SUBLANES == 0
    vp = N // nvh
    tt = T if T <= 32 else 32
    assert T % tt == 0

    def kform(x):
        x = jnp.transpose(x, (1, 3, 0, 2)).reshape(T, N, 1, bh)
        return jnp.broadcast_to(x, (T, N, nvh, bh)).reshape(T, N, L)

    kops = [kform(w), kform(-kk), kform(kk * a), kform(k), kform(r)]
    vop = jnp.transpose(v, (1, 3, 0, 2)).reshape(T, nvh, vp, bh)
    vop = jnp.transpose(vop, (0, 2, 1, 3)).reshape(T, vp, L)
    s0t = jnp.transpose(s0, (2, 3, 0, 1)).reshape(nvh, vp, N, bh)
    s0t = jnp.transpose(s0t, (1, 2, 0, 3)).reshape(vp, N, L)
    kern = functools.partial(_wkv_kernel, tt=tt, vp=vp, kd=N)
    y, sf = pl.pallas_call(
        kern,
        grid=(L // LANES, T // tt),
        in_specs=[pl.BlockSpec((tt, N, LANES), lambda l, j: (j, 0, l))] * 5 + [
                  pl.BlockSpec((tt, vp, LANES), lambda l, j: (j, 0, l)),
                  pl.BlockSpec((vp, N, LANES), lambda l, j: (0, 0, l))],
        out_specs=[pl.BlockSpec((tt, vp, LANES), lambda l, j: (j, 0, l)),
                   pl.BlockSpec((vp, N, LANES), lambda l, j: (0, 0, l))],
        out_shape=[jax.ShapeDtypeStruct((T, vp, L), jnp.float32),
                   jax.ShapeDtypeStruct((vp, N, L), jnp.float32)],
        compiler_params=pltpu.CompilerParams(dimension_semantics=("parallel", "arbitrary"),
                                             vmem_limit_bytes=48 * 1024 * 1024),
        name="wkv_scan",
    )(*kops, vop, s0t)
    y = jnp.transpose(y.reshape(T, vp, nvh, B, H), (3, 0, 4, 2, 1)).reshape(B, T, H, N)
    sf = jnp.transpose(sf.reshape(vp, N, nvh, B, H), (3, 4, 2, 0, 1)).reshape(B, H, N, N)
    return y, sf


def _rwkv_mixer(pb, shift0, s0, mu, w0, w_w2, a0, w_a2, w_g2, k_k, k_a, r_k, lnx_g, lnx_b):
    B, T, _ = pb.shape
    pb = pb.astype(jnp.float32)
    prev = jnp.concatenate([shift0[:, None, :].astype(jnp.float32), pb[:, :-1]], axis=1)
    xm = pb + (prev - pb) * mu
    c = C_RWKV
    r, k, v = xm[..., :c], xm[..., c:2 * c], xm[..., 2 * c:3 * c]
    o = 3 * c
    wl = xm[..., o:o + LORA_DECAY]
    al = xm[..., o + LORA_DECAY:o + LORA_DECAY + LORA_ICLR]
    gl = xm[..., o + LORA_DECAY + LORA_ICLR:]
    w_log = -jax.nn.softplus(-(w0 + jnp.tanh(wl) @ w_w2)) - 0.5
    decay = jnp.exp(-jnp.exp(w_log))
    a = jax.nn.sigmoid(a0 + al @ w_a2)
    g = jax.nn.sigmoid(gl) @ w_g2

    def heads(t):
        return t.reshape(B, T, N_HEADS_RWKV, HEAD_DIM)
    kk = heads(k * k_k)
    kk = kk / jnp.maximum(jnp.sqrt(jnp.sum(kk * kk, axis=-1, keepdims=True)), 1e-12)
    k = k * (1.0 + (a - 1.0) * k_a)
    r_h, k_h, v_h = heads(r), heads(k), heads(v)
    y, s_fin = _wkv_scan(r_h, heads(decay), k_h, v_h, kk, heads(a), s0.astype(jnp.float32))
    mean = jnp.mean(y, axis=-1, keepdims=True)
    var = jnp.mean(jnp.square(y - mean), axis=-1, keepdims=True)
    y = ((y - mean) * lax.rsqrt(var + GN_EPS)).reshape(B, T, C_RWKV) * lnx_g + lnx_b
    bonus = jnp.sum(r_h * k_h * r_k, axis=-1, keepdims=True) * v_h
    y = (y + bonus.reshape(B, T, C_RWKV)) * g
    return y, s_fin, pb[:, -1]


def _gelu_exact(x):
    return 0.5 * x * (1.0 + lax.erf(x * (2.0 ** -0.5)))


PEER_TOK = 8


def _peer_expert_kernel(*refs, rows, dim, route_keys):
    if route_keys:
        eidc_ref, eidn_ref, h_ref, gate_ref, tab_ref, hb_ref, wq_ref, sk_ref, out_ref, eidb_ref, gateb_ref, buf, sem = refs
    else:
        eidc_ref, eidn_ref, h_ref, gate_ref, tab_ref, out_ref, buf, sem = refs
    i = pl.program_id(0)
    n = pl.num_programs(0)
    tok = PEER_TOK
    chunks = dim // LANES
    erow = chunks
    groups = rows // SUBLANES
    rs = erow + 1

    def start_row_copy(eid_ref, src_t, t, r, slot):
        e = eid_ref[src_t, r]
        src = tab_ref.at[pl.ds(pl.multiple_of(e * erow, erow), erow)]
        dst = buf.at[pl.ds(((slot * tok + t) * rows + r) * rs, erow)]
        pltpu.make_async_copy(src, dst, sem.at[slot]).start(priority=r % 2)

    def wait_slot(slot):
        region = buf.at[pl.ds(0, tok * rows * erow)]
        pltpu.make_async_copy(region, region, sem.at[slot]).wait()

    def compute(t_blk, t, slot):
        r0 = (slot * tok + t) * rows * rs

        def words(g, k):
            return buf[pl.ds(r0 + g * SUBLANES * rs + k, SUBLANES, stride=rs), :]
        hb = [jnp.broadcast_to(h_ref[t_blk:t_blk + 1, k * LANES:(k + 1) * LANES], (SUBLANES, LANES))
              for k in range(chunks)]
        lane = lax.broadcasted_iota(jnp.int32, (SUBLANES, LANES), 1)
        hi_mask = jnp.uint32(0xFFFF0000)
        s_tile = jnp.zeros((SUBLANES, LANES), jnp.float32)
        for g in range(groups):
            acc = None
            for k in range(chunks):
                p = pltpu.bitcast(words(g, k) & hi_mask, jnp.float32) * hb[k]
                acc = p if acc is None else acc + p
            s_tile = jnp.where(lane == g, jnp.sum(acc, axis=1, keepdims=True), s_tile)
        w_tile = _gelu_exact(s_tile) * gate_ref[t_blk]
        accs = [None] * chunks
        for g in range(groups):
            wg = jnp.broadcast_to(w_tile[:, g:g + 1], (SUBLANES, LANES))
            for k in range(chunks):
                p = wg * pltpu.bitcast(words(g, k) << 16, jnp.float32)
                accs[k] = p if accs[k] is None else accs[k] + p
        out_ref[t_blk:t_blk + 1, :] = jnp.concatenate([jnp.sum(a, axis=0, keepdims=True) for a in accs], axis=1)

    @pl.when(i == 0)
    def _():
        for t in range(tok):
            for r in range(rows):
                start_row_copy(eidc_ref, t, t, r, 0)

    wait_slot(0)
    if route_keys:
        _route_head(hb_ref, wq_ref, sk_ref, eidb_ref, gateb_ref, i % wq_ref.shape[0], keys=route_keys, topk=PEER_TOPK)
    for t in range(tok):
        for r in range(rows):
            start_row_copy(eidc_ref, tok + t, t, r, 1)
        compute(t, t, 0)
    wait_slot(1)
    for t in range(tok):
        for r in range(rows):
            start_row_copy(eidn_ref, t, t, r, 0)
        compute(tok + t, t, 1)

    @pl.when(i == n - 1)
    def _():
        wait_slot(0)


PEER_PAIRS = tuple((a, b) for a in range(PEER_TOPK) for b in range(PEER_TOPK) if (a + 1) * (b + 1) <= PEER_TOPK)


def _top_rows(x, pos, k):
    vals, idxs = [], []
    for _ in range(k):
        m = jnp.max(x, axis=0, keepdims=True)
        i = jnp.min(jnp.where(x == m, pos, jnp.float32(1e9)), axis=0, keepdims=True)
        vals.append(m)
        idxs.append(i)
        x = jnp.where(pos == i, -jnp.inf, x)
    return vals, idxs


def _head_topk(s1, s2, keys, topk):
    tb = s1.shape[1]
    key_pos = lax.broadcasted_iota(jnp.int32, (keys, tb), 0).astype(jnp.float32)
    nrow = -(-len(PEER_PAIRS) // SUBLANES) * SUBLANES
    pair_pos = lax.broadcasted_iota(jnp.int32, (nrow, tb), 0).astype(jnp.float32)
    v1, i1 = _top_rows(s1, key_pos, topk)
    v2, i2 = _top_rows(s2, key_pos, topk)
    cand = jnp.full((nrow, tb), -jnp.inf, jnp.float32)
    cid = jnp.zeros((nrow, tb), jnp.float32)
    for p, (a, b) in enumerate(PEER_PAIRS):
        cand = jnp.where(pair_pos == p, v1[a] + v2[b], cand)
        cid = jnp.where(pair_pos == p, i1[a] * keys + i2[b], cid)
    top_s, top_p = _top_rows(cand, pair_pos, topk)
    es = [jnp.exp(s - top_s[0]) for s in top_s]
    den = es[0]
    for e in es[1:]:
        den = den + e
    eids = [jnp.sum(jnp.where(pair_pos == top_p[r], cid, 0.0), axis=0, keepdims=True) for r in range(topk)]
    return (jnp.concatenate(eids, axis=0).astype(jnp.int32), jnp.concatenate([e / den for e in es], axis=0))


def _route_head(hb_ref, wq_ref, sk_ref, eid_ref, gate_ref, head, *, keys, topk):
    q = jnp.dot(hb_ref[...].astype(jnp.bfloat16), wq_ref[head], preferred_element_type=jnp.float32)
    half = sk_ref.shape[3]
    nt = (((1,), (1,)), ((), ()))
    s = [lax.dot_general(sk_ref[head, p], q[:, p * half:(p + 1) * half].astype(jnp.bfloat16), nt,
                         preferred_element_type=jnp.float32) for p in range(2)]
    eid, gate = _head_topk(s[0], s[1], keys, topk)
    rows = pl.ds(pl.multiple_of(head * topk, topk), topk)
    eid_ref[rows, :] = eid
    gate_ref[rows, :] = gate


def _route_kernel(hb_ref, wq_ref, sk_ref, eid_ref, gate_ref, *, keys, topk):
    for head in range(wq_ref.shape[0]):
        _route_head(hb_ref, wq_ref, sk_ref, eid_ref, gate_ref, head, keys=keys, topk=topk)


def _route_specs(wq, sk, tb, block_of_step):
    heads = wq.shape[0]
    const = lambda a: pl.BlockSpec(a.shape, lambda i: (0,) * a.ndim)
    ins = [pl.BlockSpec((tb, wq.shape[1]), lambda i: (block_of_step(i), 0)), const(wq), const(sk)]
    outs = [pl.BlockSpec((heads * PEER_TOPK, tb), lambda i: (0, block_of_step(i)))] * 2
    return ins, outs


def _route_rows(eid_t, gate_t):
    n_tok = eid_t.shape[1]
    groups = eid_t.shape[0] // SUBLANES
    gate_tile = jnp.transpose(gate_t.reshape(groups, SUBLANES, n_tok), (2, 1, 0))
    return eid_t.T, jnp.pad(gate_tile, ((0, 0), (0, 0), (0, LANES - groups)))


PEER_STAGES = 4


def _peer_ffn(h, w_pq, sub_keys, expert_u, expert_v, *, tb=LANES):
    N, D = h.shape
    heads, _, keys, half = sub_keys.shape
    E = expert_u.shape[0]
    R = heads * PEER_TOPK
    chunks = D // LANES
    step_tok = 2 * PEER_TOK
    part = N // PEER_STAGES
    assert N % PEER_STAGES == 0 and part % tb == 0 and tb == heads * step_tok and D % LANES == 0
    wq = jnp.transpose(w_pq.reshape(D, heads, 2 * half), (1, 0, 2)).astype(jnp.bfloat16)
    sk = sub_keys.astype(jnp.bfloat16)

    def bf16_bits(x):
        return lax.bitcast_convert_type(x.astype(jnp.bfloat16), jnp.uint16).astype(jnp.uint32)
    tab = ((bf16_bits(expert_u) << 16) | bf16_bits(expert_v)).reshape(E * chunks, LANES)
    route_shapes = [jax.ShapeDtypeStruct((R, part), jnp.int32), jax.ShapeDtypeStruct((R, part), jnp.float32)]
    parts = [h[g * part:(g + 1) * part] for g in range(PEER_STAGES)]
    ins, outs = _route_specs(wq, sk, tb, lambda i: i)
    routed = pl.pallas_call(
        functools.partial(_route_kernel, keys=keys, topk=PEER_TOPK),
        grid=(part // tb,), in_specs=ins, out_specs=outs, out_shape=route_shapes,
        compiler_params=pltpu.CompilerParams(dimension_semantics=("parallel",), vmem_limit_bytes=48 * 1024 * 1024),
        name="peer_route",
    )(parts[0], wq, sk)
    nsteps = part // step_tok
    fetch_in = [
        pl.BlockSpec((step_tok, R), lambda i: (i, 0), memory_space=pltpu.SMEM),
        pl.BlockSpec((step_tok, R), lambda i: (jnp.minimum(i + 1, nsteps - 1), 0), memory_space=pltpu.SMEM),
        pl.BlockSpec((step_tok, D), lambda i: (i, 0)),
        pl.BlockSpec((step_tok, SUBLANES, LANES), lambda i: (i, 0, 0)),
        pl.BlockSpec(memory_space=pl.ANY),
    ]
    fetch_out = pl.BlockSpec((step_tok, D), lambda i: (i, 0))
    scratch = [pltpu.VMEM((2 * PEER_TOK * R * (chunks + 1), LANES), jnp.uint32), pltpu.SemaphoreType.DMA((2,))]
    params = pltpu.CompilerParams(dimension_semantics=("arbitrary",), vmem_limit_bytes=56 * 1024 * 1024,
                                  disable_bounds_checks=True)
    ffn = []
    for g in range(PEER_STAGES):
        eid, gate_tile = _route_rows(*routed)
        if g + 1 < PEER_STAGES:
            ins, outs = _route_specs(wq, sk, tb, lambda i: i // heads)
            res = pl.pallas_call(
                functools.partial(_peer_expert_kernel, rows=R, dim=D, route_keys=keys),
                grid=(nsteps,), in_specs=fetch_in + ins, out_specs=[fetch_out] + outs,
                out_shape=[jax.ShapeDtypeStruct((part, D), jnp.float32)] + route_shapes,
                scratch_shapes=scratch, compiler_params=params, name="peer_experts_route",
            )(eid, eid, parts[g], gate_tile, tab, parts[g + 1], wq, sk)
            ffn.append(res[0])
            routed = res[1:]
        else:
            ffn.append(pl.pallas_call(
                functools.partial(_peer_expert_kernel, rows=R, dim=D, route_keys=0),
                grid=(nsteps,), in_specs=fetch_in, out_specs=fetch_out,
                out_shape=jax.ShapeDtypeStruct((part, D), jnp.float32),
                scratch_shapes=scratch, compiler_params=params, name="peer_experts",
            )(eid, eid, parts[g], gate_tile, tab))
    return jnp.concatenate(ffn, axis=0)


def _mix_proj_kernel(oa_ref, yr_ref, x_ref, g1_ref, sc_ref, sh_ref, gn_ref, wa_ref, wr_ref, x1_ref, h2_ref):
    mix = (jnp.dot(oa_ref[0].astype(jnp.bfloat16), wa_ref[...], preferred_element_type=jnp.float32)
           + jnp.dot(yr_ref[0].astype(jnp.bfloat16), wr_ref[...], preferred_element_type=jnp.float32))
    x1 = x_ref[0] + g1_ref[0] * mix
    x1_ref[0] = x1
    h = x1 * lax.rsqrt(jnp.mean(x1 * x1, axis=-1, keepdims=True) + NORM_EPS) * gn_ref[...]
    h2_ref[0] = h * (1.0 + sc_ref[0]) + sh_ref[0]


def _mix_proj(o_attn, y_rwkv, x, g1, sc2, sh2, gain2, w_out):
    B, T, D = x.shape
    ca, cr = o_attn.shape[2], y_rwkv.shape[2]
    tm = min(T, 512)
    assert T % tm == 0
    blk = lambda w: pl.BlockSpec((1, tm, w), lambda b, t: (b, t, 0))
    mod = _mod_spec(g1, tm)
    const = lambda a: pl.BlockSpec(a.shape, lambda b, t: (0,) * a.ndim)
    wa, wr = w_out[:ca].astype(jnp.bfloat16), w_out[ca:].astype(jnp.bfloat16)
    gn = gain2.reshape(1, D)
    return pl.pallas_call(
        _mix_proj_kernel,
        grid=(B, T // tm),
        in_specs=[blk(ca), blk(cr), blk(D), mod, mod, mod, const(gn), const(wa), const(wr)],
        out_specs=[blk(D), blk(D)],
        out_shape=[jax.ShapeDtypeStruct((B, T, D), jnp.float32)] * 2,
        compiler_params=pltpu.CompilerParams(dimension_semantics=("parallel", "parallel"),
                                             vmem_limit_bytes=48 * 1024 * 1024),
        name="mix_proj",
    )(o_attn, y_rwkv, x, g1, sc2, sh2, gn, wa, wr)


def _resid_kernel(x_ref, g_ref, f_ref, o_ref):
    o_ref[...] = x_ref[...] + g_ref[...] * f_ref[...]


def _resid(x, g, f):
    B, T, D = x.shape
    tb = min(T, 512)
    return pl.pallas_call(
        _resid_kernel,
        grid=(B, T // tb),
        in_specs=[pl.BlockSpec((1, tb, D), lambda b, t: (b, t, 0)),
                  pl.BlockSpec((1, 1, D), lambda b, t: (b, 0, 0)),
                  pl.BlockSpec((1, tb, D), lambda b, t: (b, t, 0))],
        out_specs=pl.BlockSpec((1, tb, D), lambda b, t: (b, t, 0)),
        out_shape=jax.ShapeDtypeStruct(x.shape, x.dtype),
        name="ffn_residual",
    )(x, g, f)


def _layer(x, c, k_buf, v_buf, wkv0, shift0, rel_bias, p):
    B, T, _ = x.shape
    mod = jax.nn.silu(c.astype(jnp.float32)) @ p['ada_w'] + p['ada_b']
    sh1, sc1, g1, sh2, sc2, g2 = jnp.split(mod[:, None, :], 6, axis=-1)
    if T < SUBLANES:
        fold = lambda a: a.reshape(1, B * T, a.shape[-1])
        per_token = lambda m: fold(jnp.broadcast_to(m, (B, T, D_MODEL)))
    else:
        fold = per_token = lambda a: a
    unfold = lambda a: a.reshape(B, T, a.shape[-1])
    q3, k3, v3, proj_rwkv = (unfold(a) for a in _in_proj(
        fold(x), p['norm1_g'], per_token(sc1), per_token(sh1), p['w_in'], p['q_norm_g'], p['k_norm_g']))
    q, k, v = (a.reshape(B, T, N_HEADS_ATTN, HEAD_DIM) for a in (q3, k3, v3))
    if k_buf is None:
        o_attn = _prompt_attn(q3, k3, v3, rel_bias)
        keep = min(MAX_WINDOW, T)
        k_rows, v_rows = k[:, T - keep:], v[:, T - keep:]
        wkv0 = jnp.zeros((B, N_HEADS_RWKV, HEAD_DIM, HEAD_DIM), jnp.float32)
        shift0 = jnp.zeros((B, COLS_RWKV), jnp.float32)
    else:
        o_attn = _sample_attn(q, k, v, jnp.transpose(k_buf, (0, 2, 3, 1)), jnp.transpose(v_buf, (0, 2, 3, 1)), rel_bias)
        k_rows, v_rows = k, v
    y_rwkv, s_fin, shift_new = _rwkv_mixer(
        proj_rwkv, shift0, wkv0, p['mu_shift'], p['w0'], p['w_w2'], p['a0'], p['w_a2'],
        p['w_g2'], p['k_k'], p['k_a'], p['r_k'], p['lnx_g'], p['lnx_b'])
    x, h2 = (unfold(a) for a in _mix_proj(
        fold(o_attn.reshape(B, T, C_ATTN)), fold(y_rwkv), fold(x), per_token(g1), per_token(sc2), per_token(sh2),
        p['norm2_g'], p['w_out']))
    return x, h2, g2, k_rows, v_rows, s_fin, shift_new


def kernel(x_prompt, x_sample, c_prompt, c_sample, cache_k_win, cache_v_win, state_wkv, state_shift,
           ada_w, ada_b, norm1_g, norm2_g, w_in, q_norm_g, k_norm_g, rel_bias, mu_shift, w0, w_w2, a0,
           w_a2, w_g2, k_k, k_a, r_k, lnx_g, lnx_b, w_out, w_peer_q, peer_sub_keys, expert_u, expert_v):
    xp, xs = x_prompt, x_sample
    kp_l, vp_l, sp_l, hp_l = [], [], [], []
    ks_l, vs_l, ss_l, hs_l = [], [], [], []
    names = ('ada_w', 'ada_b', 'norm1_g', 'norm2_g', 'w_in', 'q_norm_g', 'k_norm_g', 'mu_shift', 'w0', 'w_w2',
             'a0', 'w_a2', 'w_g2', 'k_k', 'k_a', 'r_k', 'lnx_g', 'lnx_b', 'w_out', 'w_peer_q', 'peer_sub_keys',
             'expert_u', 'expert_v')
    vals = (ada_w, ada_b, norm1_g, norm2_g, w_in, q_norm_g, k_norm_g, mu_shift, w0, w_w2, a0, w_a2, w_g2, k_k,
            k_a, r_k, lnx_g, lnx_b, w_out, w_peer_q, peer_sub_keys, expert_u, expert_v)
    for l in range(DEPTH):
        p = {n: v[l] for n, v in zip(names, vals)}
        xp, h2p, g2p, kp, vp, sp, hp = _layer(xp, c_prompt, None, None, None, None, rel_bias, p)
        xs, h2s, g2s, kn, vn, sn, hn = _layer(xs, c_sample, cache_k_win[l], cache_v_win[l], state_wkv[l],
                                              state_shift[l], rel_bias, p)
        n_p = h2p.shape[0] * h2p.shape[1]
        h2 = jnp.concatenate([h2p.reshape(-1, D_MODEL), h2s.reshape(-1, D_MODEL)], axis=0)
        ffn = _peer_ffn(h2, p['w_peer_q'], p['peer_sub_keys'], p['expert_u'], p['expert_v'])
        xp = _resid(xp, g2p, ffn[:n_p].reshape(xp.shape))
        xs = _resid(xs, g2s, ffn[n_p:].reshape(xs.shape))
        kp_l.append(kp); vp_l.append(vp); sp_l.append(sp); hp_l.append(hp)
        ks_l.append(kn); vs_l.append(vn); ss_l.append(sn); hs_l.append(hn)
    return (xp, xs, jnp.stack(kp_l), jnp.stack(vp_l), jnp.stack(sp_l), jnp.stack(hp_l),
            jnp.stack(ks_l), jnp.stack(vs_l), jnp.stack(ss_l), jnp.stack(hs_l))
```

```python
import functools
import math
import jax, jax.numpy as jnp
from jax import lax
import numpy as np
from jax.experimental import pallas as pl
from jax.experimental.pallas import tpu as pltpu

D_MODEL = 1024
DEPTH = 1
HEAD_DIM = 64
N_HEADS_ATTN = 8
N_HEADS_RWKV = 8
C_ATTN = N_HEADS_ATTN * HEAD_DIM
C_RWKV = N_HEADS_RWKV * HEAD_DIM
DILATIONS = ((128, 1), (512, 4), (2048, 16))
MAX_WINDOW = 2048
N_BUCKETS = 32
MAX_DISTANCE = 2048
LORA_DECAY = 32
LORA_ICLR = 32
LORA_GATE = 64
COLS_RWKV = 3 * C_RWKV + LORA_DECAY + LORA_ICLR + LORA_GATE
D_IN = 3 * C_ATTN + COLS_RWKV
PEER_TOPK = 16
NORM_EPS = 1e-6
GN_EPS = 64e-5
NEG_INF = -1e30
ATTN_SCALE = HEAD_DIM ** -0.5


def _t5_bucket(dist):
    dist = np.asarray(dist, dtype=np.int64)
    max_exact = N_BUCKETS // 2
    safe = np.maximum(dist, 1) / max_exact
    large = max_exact + (np.log(safe) / math.log(MAX_DISTANCE / max_exact) * (N_BUCKETS - max_exact)).astype(np.int64)
    large = np.minimum(large, N_BUCKETS - 1)
    return np.where(dist < max_exact, dist, large).astype(np.int32)


LANES = 128
SUBLANES = 8


def _mod_spec(a, tm):
    per_token = a.shape[1] != 1
    return pl.BlockSpec((1, tm if per_token else 1, a.shape[2]), lambda b, t: (b, t if per_token else 0, 0))


def _segment_mean_matrix(width, seg):
    lane = np.arange(width)
    return jnp.asarray((lane[:, None] // seg == lane[None, :] // seg).astype(np.float32) / seg, jnp.bfloat16)


def _head_rms(x, seg_ref, gain):
    x2 = x * x
    hi = x2.astype(jnp.bfloat16)
    lo = (x2 - hi.astype(jnp.float32)).astype(jnp.bfloat16)
    ms = (jnp.dot(hi, seg_ref[...], preferred_element_type=jnp.float32)
          + jnp.dot(lo, seg_ref[...], preferred_element_type=jnp.float32))
    return x * lax.rsqrt(ms + NORM_EPS) * gain


def _in_proj_kernel(x_ref, g_ref, sc_ref, sh_ref, w_ref, seg_ref, gq_ref, gk_ref, q_ref, k_ref, v_ref, r_ref):
    x = x_ref[0]
    h = x * lax.rsqrt(jnp.mean(x * x, axis=-1, keepdims=True) + NORM_EPS) * g_ref[...]
    h = (h * (1.0 + sc_ref[0]) + sh_ref[0]).astype(jnp.bfloat16)
    c = q_ref.shape[2]
    q_ref[0] = _head_rms(jnp.dot(h, w_ref[:, :c], preferred_element_type=jnp.float32), seg_ref, gq_ref[...])
    k_ref[0] = _head_rms(jnp.dot(h, w_ref[:, c:2 * c], preferred_element_type=jnp.float32), seg_ref, gk_ref[...])
    v_ref[0] = jnp.dot(h, w_ref[:, 2 * c:3 * c], preferred_element_type=jnp.float32)
    r_ref[0] = jnp.dot(h, w_ref[:, 3 * c:], preferred_element_type=jnp.float32)


def _in_proj(x, gain, sc, sh, w_in, q_gain, k_gain):
    B, T, D = x.shape
    n_out = w_in.shape[1]
    c = C_ATTN
    tm = min(T, 512)
    assert T % tm == 0
    heads = c // HEAD_DIM
    row = lambda a: a.reshape(1, -1)
    blk = lambda w: pl.BlockSpec((1, tm, w), lambda b, t: (b, t, 0))
    const = lambda a: pl.BlockSpec(a.shape, lambda b, t: (0,) * a.ndim)
    mod = _mod_spec(sc, tm)
    seg = _segment_mean_matrix(c, HEAD_DIM)
    wb = w_in.astype(jnp.bfloat16)
    gq, gk = row(jnp.tile(q_gain, heads)), row(jnp.tile(k_gain, heads))
    return pl.pallas_call(
        _in_proj_kernel,
        grid=(B, T // tm),
        in_specs=[blk(D), const(row(gain)), mod, mod, const(wb), const(seg), const(gq), const(gk)],
        out_specs=[blk(c), blk(c), blk(c), blk(n_out - 3 * c)],
        out_shape=[jax.ShapeDtypeStruct((B, T, c), jnp.float32)] * 3
                  + [jax.ShapeDtypeStruct((B, T, n_out - 3 * c), jnp.float32)],
        compiler_params=pltpu.CompilerParams(dimension_semantics=("parallel", "parallel"),
                                             vmem_limit_bytes=56 * 1024 * 1024),
        name="in_proj",
    )(x, row(gain), sc, sh, wb, seg, gq, gk)


def _sample_attn_kernel(q_ref, kn_ref, vn_ref, k_ref, v_ref, b1_ref, b2_ref, b3_ref, bn_ref, o_ref,
                        d_scr, dn_scr, p_scr, pn_scr):
    _, H, S, C = q_ref.shape
    P = k_ref.shape[3]
    for h in range(H):
        rows = pl.ds(h * S, S)
        d_scr[rows, :] = jnp.dot(q_ref[0, h], k_ref[0, h], preferred_element_type=jnp.float32) * ATTN_SCALE
        dn_scr[rows, :] = jnp.dot(q_ref[0, h], kn_ref[0, h], preferred_element_type=jnp.float32) * ATTN_SCALE
    d, dn = d_scr[...], dn_scr[...]
    ecs, ens, dens, lses = [], [], [], []
    for i, b_ref in enumerate((b1_ref, b2_ref, b3_ref)):
        w = b_ref.shape[1]
        lc = d[:, P - w:] + b_ref[...]
        ln = dn + bn_ref[i]
        m = jnp.maximum(jnp.max(lc, axis=1, keepdims=True), jnp.max(ln, axis=1, keepdims=True))
        ecs.append(jnp.exp(lc - m))
        ens.append(jnp.exp(ln - m))
        dens.append(jnp.sum(ecs[i], axis=1, keepdims=True) + jnp.sum(ens[i], axis=1, keepdims=True))
        lses.append(m + jnp.log(dens[i]))
    mm = jnp.maximum(jnp.maximum(lses[0], lses[1]), lses[2])
    ws = [jnp.exp(l - mm) for l in lses]
    wsum = ws[0] + ws[1] + ws[2]
    coef = [ws[i] / (wsum * dens[i]) for i in range(3)]
    w1, w2 = b1_ref.shape[1], b2_ref.shape[1]
    p3 = ecs[2] * coef[2]
    p2 = ecs[1] * coef[1]
    p_scr[:, :P - w2] = p3[:, :P - w2]
    p_scr[:, P - w2:P - w1] = p3[:, P - w2:P - w1] + p2[:, :w2 - w1]
    p_scr[:, P - w1:] = p3[:, P - w1:] + p2[:, w2 - w1:] + ecs[0] * coef[0]
    pn_scr[...] = ens[0] * coef[0] + ens[1] * coef[1] + ens[2] * coef[2]
    lane = lax.broadcasted_iota(jnp.int32, (C, S), 1)
    for h in range(H):
        o_tile = jnp.zeros((C, S), jnp.float32)
        for s in range(S):
            r = h * S + s
            acc = None
            for j in range(P // LANES):
                t = v_ref[0, h, :, j * LANES:(j + 1) * LANES] * p_scr[r:r + 1, j * LANES:(j + 1) * LANES]
                acc = t if acc is None else acc + t
            col = jnp.sum(acc, axis=1, keepdims=True) + jnp.sum(vn_ref[0, h] * pn_scr[r:r + 1, :], axis=1, keepdims=True)
            o_tile = jnp.where(lane == s, col, o_tile)
        o_ref[0, h] = o_tile


def _bias_by_distance(rel_bias, dist, valid):
    onehot = np.eye(N_BUCKETS, dtype=np.float32)[_t5_bucket(np.maximum(dist, 0))]
    b = jnp.dot(onehot, rel_bias.astype(jnp.float32), precision=lax.Precision.HIGHEST)
    return jnp.where(valid[:, None], b, NEG_INF)


def _sample_bias_tables(rel_bias, S, lb):
    tabs, news = [], []
    H = rel_bias.shape[1]
    for window, dil in DILATIONS:
        d = np.arange(window + S + 1)
        vec = _bias_by_distance(rel_bias, d, (d % dil == 0) & (d >= 1) & (d <= window))
        rows = [vec[s + 1:s + 1 + window][::-1] for s in range(S)]
        tabs.append(jnp.transpose(jnp.stack(rows), (2, 0, 1)).reshape(H * S, window))
        dn = (np.arange(S)[:, None] - np.arange(S)[None, :]).reshape(-1)
        bn = _bias_by_distance(rel_bias, dn, (dn >= 0) & (dn % dil == 0) & (dn <= window)).reshape(S, S, H)
        news.append(jnp.transpose(bn, (2, 0, 1)).reshape(H * S, S))
    return tabs, jnp.stack(news)


def _sample_attn(q, k_new, v_new, k_buf_t, v_buf_t, rel_bias):
    B, S, H, C = q.shape
    lb = k_buf_t.shape[3]
    assert lb >= DILATIONS[-1][0] and all(w % LANES == 0 for w, _ in DILATIONS)
    tabs, bn = _sample_bias_tables(rel_bias, S, lb)
    tr = lambda a: jnp.transpose(a, (0, 2, 3, 1))
    qs = pl.BlockSpec((1, H, S, C), lambda b: (b, 0, 0, 0))
    tok = pl.BlockSpec((1, H, C, S), lambda b: (b, 0, 0, 0))
    cache = pl.BlockSpec((1, H, C, lb), lambda b: (b, 0, 0, 0))
    full = lambda a: pl.BlockSpec(a.shape, lambda b: (0,) * a.ndim)
    o = pl.pallas_call(
        _sample_attn_kernel,
        grid=(B,),
        in_specs=[qs, tok, tok, cache, cache, full(tabs[0]), full(tabs[1]), full(tabs[2]), full(bn)],
        out_specs=tok,
        out_shape=jax.ShapeDtypeStruct((B, H, C, S), jnp.float32),
        scratch_shapes=[pltpu.VMEM((H * S, lb), jnp.float32), pltpu.VMEM((H * S, S), jnp.float32),
                        pltpu.VMEM((H * S, lb), jnp.float32), pltpu.VMEM((H * S, S), jnp.float32)],
        compiler_params=pltpu.CompilerParams(dimension_semantics=("parallel",),
                                             vmem_limit_bytes=48 * 1024 * 1024),
        name="sample_attn",
    )(jnp.transpose(q, (0, 2, 1, 3)), tr(k_new), tr(v_new), k_buf_t, v_buf_t, *tabs, bn)
    return jnp.transpose(o, (0, 3, 1, 2))


def _prompt_attn_kernel(q_ref, kp_ref, kc_ref, vp_ref, vc_ref, bias_ref, o_ref, lse_ref):
    g = pl.program_id(2)
    n = q_ref.shape[1]
    heads = bias_ref.shape[0]
    lane = lax.broadcasted_iota(jnp.int32, (n, LANES), 1)
    nt = (((1,), (1,)), ((), ()))
    per = LANES // HEAD_DIM
    sls = [slice(hp * LANES, (hp + 1) * LANES) for hp in range(heads // per)]
    keeps = [(lane >= half * HEAD_DIM) & (lane < (half + 1) * HEAD_DIM) for half in range(per)]
    logits = []
    for h in range(heads):
        sl, keep = sls[h // per], keeps[h % per]
        qh = jnp.where(keep, q_ref[0, :, sl], 0.0)
        lp = lax.dot_general(qh, kp_ref[0, :, sl], nt, preferred_element_type=jnp.float32) * ATTN_SCALE + bias_ref[h, :, :n]
        lc = lax.dot_general(qh, kc_ref[0, :, sl], nt, preferred_element_type=jnp.float32) * ATTN_SCALE + bias_ref[h, :, n:]
        logits.append((jnp.where(g == 0, NEG_INF, lp), lc))
    probs = []
    for lp, lc in logits:
        m = jnp.max(jnp.maximum(lp, lc), axis=-1, keepdims=True)
        ep = jnp.exp(lp - m)
        ec = jnp.exp(lc - m)
        s = jnp.sum(ep + ec, axis=-1, keepdims=True)
        probs.append((ep, ec, s, m + jnp.log(s)))
    for hp, sl in enumerate(sls):
        o2 = jnp.zeros((n, LANES), jnp.float32)
        l2 = jnp.zeros((n, LANES), jnp.float32)
        for half in range(per):
            ep, ec, s, lse = probs[hp * per + half]
            o = (jnp.dot(ep, vp_ref[0, :, sl], preferred_element_type=jnp.float32)
                 + jnp.dot(ec, vc_ref[0, :, sl], preferred_element_type=jnp.float32)) / s
            o2 = jnp.where(keeps[half], o, o2)
            l2 = jnp.where(keeps[half], lse, l2)
        o_ref[0, :, sl] = o2
        lse_ref[0, :, sl] = l2


def _merge_kernel(o1, o2, o3, l1, l2, l3, out):
    m = jnp.maximum(jnp.maximum(l1[...], l2[...]), l3[...])
    w1, w2, w3 = jnp.exp(l1[...] - m), jnp.exp(l2[...] - m), jnp.exp(l3[...] - m)
    ws = w1 + w2 + w3
    out[...] = (w1 / ws) * o1[...] + (w2 / ws) * o2[...] + (w3 / ws) * o3[...]


def _prompt_bias(rel_bias, n, dil):
    H = rel_bias.shape[1]
    m = 3 * n
    j = 2 * n - 1 - np.arange(m)
    u = _bias_by_distance(rel_bias, j * dil, (j >= 0) & (j <= n))
    rows = jnp.tile(u, (n, 1))[:n * (m - 1)].reshape(n, m - 1, H)
    return jnp.transpose(rows[:, n - 1:3 * n - 1], (2, 0, 1))


def _prompt_attn(q, k, v, rel_bias):
    B, T, HC = q.shape
    H = HC // HEAD_DIM
    outs, lses = [], []
    for window, dil in DILATIONS:
        n = window // dil
        assert T % window == 0 and HC % LANES == 0
        G = T // window
        view = lambda a: a.reshape(B, T // dil, dil * HC)
        cur = pl.BlockSpec((1, n, HC), lambda b, r, g: (b, g, r))
        prev = pl.BlockSpec((1, n, HC), lambda b, r, g: (b, jnp.maximum(g - 1, 0), r))
        bias = _prompt_bias(rel_bias, n, dil)
        o, lse = pl.pallas_call(
            _prompt_attn_kernel,
            grid=(B, dil, G),
            in_specs=[cur, prev, cur, prev, cur, pl.BlockSpec((H, n, 2 * n), lambda b, r, g: (0, 0, 0))],
            out_specs=[cur, cur],
            out_shape=[jax.ShapeDtypeStruct((B, T // dil, dil * HC), jnp.float32)] * 2,
            compiler_params=pltpu.CompilerParams(dimension_semantics=("parallel", "parallel", "arbitrary")),
            name="prompt_attn_d%d" % dil,
        )(view(q), view(k), view(k), view(v), view(v), bias)
        outs.append(o.reshape(B, T, HC))
        lses.append(lse.reshape(B, T, HC))
    tb = 512
    blk = pl.BlockSpec((1, tb, HC), lambda b, t: (b, t, 0))
    return pl.pallas_call(
        _merge_kernel, grid=(B, T // tb), in_specs=[blk] * 6, out_specs=blk,
        out_shape=jax.ShapeDtypeStruct((B, T, HC), jnp.float32),
        compiler_params=pltpu.CompilerParams(dimension_semantics=("parallel", "parallel")),
        name="prompt_attn_merge",
    )(*outs, *lses)


WKV_ROWS_PER_PASS = 4


def _wkv_kernel(w_ref, nkk_ref, b_ref, k_ref, r_ref, vop_ref, s0_ref, y_ref, s_ref, *, tt, vp, kd):
    j = pl.program_id(1)
    kgs = kd // SUBLANES
    G = WKV_ROWS_PER_PASS

    @pl.when(j == 0)
    def _():
        s_ref[...] = s0_ref[...]

    def step(t, c):
        for v0 in range(0, vp, G):
            accs = [None] * G
            for kg in range(kgs):
                ks = pl.ds(kg * SUBLANES, SUBLANES)
                nkk = nkk_ref[t, ks, :]
                for i in range(G):
                    p = s_ref[v0 + i, ks, :] * nkk
                    accs[i] = p if accs[i] is None else accs[i] + p
            sa = [jnp.sum(a, axis=0, keepdims=True) for a in accs]
            vv = [vop_ref[t, pl.ds(v0 + i, 1), :] for i in range(G)]
            yacc = [None] * G
            for kg in range(kgs):
                ks = pl.ds(kg * SUBLANES, SUBLANES)
                w = w_ref[t, ks, :]
                b = b_ref[t, ks, :]
                k = k_ref[t, ks, :]
                r = r_ref[t, ks, :]
                for i in range(G):
                    s2 = s_ref[v0 + i, ks, :] * w + sa[i] * b + vv[i] * k
                    s_ref[v0 + i, ks, :] = s2
                    p = s2 * r
                    yacc[i] = p if yacc[i] is None else yacc[i] + p
            for i in range(G):
                y_ref[t, pl.ds(v0 + i, 1), :] = jnp.sum(yacc[i], axis=0, keepdims=True)
        return c

    lax.fori_loop(0, tt, step, 0)


def _wkv_scan(r, w, k, v, kk, a, s0):
    B, T, H, N = r.shape
    bh = B * H
    nvh = max(1, LANES // bh)
    L = nvh * bh
    assert L % LANES == 0 and N % (nvh * WKV_ROWS_PER_PASS) == 0 and N % SUBLANES == 0
    vp = N // nvh
    tt = T if T <= 32 else 32
    assert T % tt == 0

    def kform(x):
        x = jnp.transpose(x, (1, 3, 0, 2)).reshape(T, N, 1, bh)
        return jnp.broadcast_to(x, (T, N, nvh, bh)).reshape(T, N, L)

    kops = [kform(w), kform(-kk), kform(kk * a), kform(k), kform(r)]
    vop = jnp.transpose(v, (1, 3, 0, 2)).reshape(T, nvh, vp, bh)
    vop = jnp.transpose(vop, (0, 2, 1, 3)).reshape(T, vp, L)
    s0t = jnp.transpose(s0, (2, 3, 0, 1)).reshape(nvh, vp, N, bh)
    s0t = jnp.transpose(s0t, (1, 2, 0, 3)).reshape(vp, N, L)
    kern = functools.partial(_wkv_kernel, tt=tt, vp=vp, kd=N)
    y, sf = pl.pallas_call(
        kern,
        grid=(L // LANES, T // tt),
        in_specs=[pl.BlockSpec((tt, N, LANES), lambda l, j: (j, 0, l))] * 5 + [
                  pl.BlockSpec((tt, vp, LANES), lambda l, j: (j, 0, l)),
                  pl.BlockSpec((vp, N, LANES), lambda l, j: (0, 0, l))],
        out_specs=[pl.BlockSpec((tt, vp, LANES), lambda l, j: (j, 0, l)),
                   pl.BlockSpec((vp, N, LANES), lambda l, j: (0, 0, l))],
        out_shape=[jax.ShapeDtypeStruct((T, vp, L), jnp.float32),
                   jax.ShapeDtypeStruct((vp, N, L), jnp.float32)],
        compiler_params=pltpu.CompilerParams(dimension_semantics=("parallel", "arbitrary"),
                                             vmem_limit_bytes=48 * 1024 * 1024),
        name="wkv_scan",
    )(*kops, vop, s0t)
    y = jnp.transpose(y.reshape(T, vp, nvh, B, H), (3, 0, 4, 2, 1)).reshape(B, T, H, N)
    sf = jnp.transpose(sf.reshape(vp, N, nvh, B, H), (3, 4, 2, 0, 1)).reshape(B, H, N, N)
    return y, sf


def _rwkv_mixer(pb, shift0, s0, mu, w0, w_w2, a0, w_a2, w_g2, k_k, k_a, r_k, lnx_g, lnx_b):
    B, T, _ = pb.shape
    pb = pb.astype(jnp.float32)
    prev = jnp.concatenate([shift0[:, None, :].astype(jnp.float32), pb[:, :-1]], axis=1)
    xm = pb + (prev - pb) * mu
    c = C_RWKV
    r, k, v = xm[..., :c], xm[..., c:2 * c], xm[..., 2 * c:3 * c]
    o = 3 * c
    wl = xm[..., o:o + LORA_DECAY]
    al = xm[..., o + LORA_DECAY:o + LORA_DECAY + LORA_ICLR]
    gl = xm[..., o + LORA_DECAY + LORA_ICLR:]
    w_log = -jax.nn.softplus(-(w0 + jnp.tanh(wl) @ w_w2)) - 0.5
    decay = jnp.exp(-jnp.exp(w_log))
    a = jax.nn.sigmoid(a0 + al @ w_a2)
    g = jax.nn.sigmoid(gl) @ w_g2

    def heads(t):
        return t.reshape(B, T, N_HEADS_RWKV, HEAD_DIM)
    kk = heads(k * k_k)
    kk = kk / jnp.maximum(jnp.sqrt(jnp.sum(kk * kk, axis=-1, keepdims=True)), 1e-12)
    k = k * (1.0 + (a - 1.0) * k_a)
    r_h, k_h, v_h = heads(r), heads(k), heads(v)
    y, s_fin = _wkv_scan(r_h, heads(decay), k_h, v_h, kk, heads(a), s0.astype(jnp.float32))
    mean = jnp.mean(y, axis=-1, keepdims=True)
    var = jnp.mean(jnp.square(y - mean), axis=-1, keepdims=True)
    y = ((y - mean) * lax.rsqrt(var + GN_EPS)).reshape(B, T, C_RWKV) * lnx_g + lnx_b
    bonus = jnp.sum(r_h * k_h * r_k, axis=-1, keepdims=True) * v_h
    y = (y + bonus.reshape(B, T, C_RWKV)) * g
    return y, s_fin, pb[:, -1]


def _gelu_exact(x):
    return 0.5 * x * (1.0 + lax.erf(x * (2.0 ** -0.5)))


PEER_TOK = 8


def _peer_expert_kernel(*refs, rows, dim, route_keys):
    if route_keys:
        eidc_ref, eidn_ref, h_ref, gate_ref, tab_ref, hb_ref, wq_ref, sk_ref, out_ref, eidb_ref, gateb_ref, buf, sem = refs
    else:
        eidc_ref, eidn_ref, h_ref, gate_ref, tab_ref, out_ref, buf, sem = refs
    i = pl.program_id(0)
    n = pl.num_programs(0)
    tok = PEER_TOK
    chunks = dim // LANES
    erow = chunks
    groups = rows // SUBLANES
    rs = erow + 1

    def start_row_copy(eid_ref, src_t, t, r, slot):
        e = eid_ref[src_t, r]
        src = tab_ref.at[pl.ds(pl.multiple_of(e * erow, erow), erow)]
        dst = buf.at[pl.ds(((slot * tok + t) * rows + r) * rs, erow)]
        pltpu.make_async_copy(src, dst, sem.at[slot]).start(priority=r % 2)

    def wait_slot(slot):
        region = buf.at[pl.ds(0, tok * rows * erow)]
        pltpu.make_async_copy(region, region, sem.at[slot]).wait()

    def compute(t_blk, t, slot):
        r0 = (slot * tok + t) * rows * rs

        def words(g, k):
            return buf[pl.ds(r0 + g * SUBLANES * rs + k, SUBLANES, stride=rs), :]
        hb = [jnp.broadcast_to(h_ref[t_blk:t_blk + 1, k * LANES:(k + 1) * LANES], (SUBLANES, LANES))
              for k in range(chunks)]
        lane = lax.broadcasted_iota(jnp.int32, (SUBLANES, LANES), 1)
        hi_mask = jnp.uint32(0xFFFF0000)
        s_tile = jnp.zeros((SUBLANES, LANES), jnp.float32)
        for g in range(groups):
            acc = None
            for k in range(chunks):
                p = pltpu.bitcast(words(g, k) & hi_mask, jnp.float32) * hb[k]
                acc = p if acc is None else acc + p
            s_tile = jnp.where(lane == g, jnp.sum(acc, axis=1, keepdims=True), s_tile)
        w_tile = _gelu_exact(s_tile) * gate_ref[t_blk]
        accs = [None] * chunks
        for g in range(groups):
            wg = jnp.broadcast_to(w_tile[:, g:g + 1], (SUBLANES, LANES))
            for k in range(chunks):
                p = wg * pltpu.bitcast(words(g, k) << 16, jnp.float32)
                accs[k] = p if accs[k] is None else accs[k] + p
        out_ref[t_blk:t_blk + 1, :] = jnp.concatenate([jnp.sum(a, axis=0, keepdims=True) for a in accs], axis=1)

    @pl.when(i == 0)
    def _():
        for t in range(tok):
            for r in range(rows):
                start_row_copy(eidc_ref, t, t, r, 0)

    wait_slot(0)
    if route_keys:
        _route_head(hb_ref, wq_ref, sk_ref, eidb_ref, gateb_ref, i % wq_ref.shape[0], keys=route_keys, topk=PEER_TOPK)
    for t in range(tok):
        for r in range(rows):
            start_row_copy(eidc_ref, tok + t, t, r, 1)
        compute(t, t, 0)
    wait_slot(1)
    for t in range(tok):
        for r in range(rows):
            start_row_copy(eidn_ref, t, t, r, 0)
        compute(tok + t, t, 1)

    @pl.when(i == n - 1)
    def _():
        wait_slot(0)


PEER_PAIRS = tuple((a, b) for a in range(PEER_TOPK) for b in range(PEER_TOPK) if (a + 1) * (b + 1) <= PEER_TOPK)


def _top_rows(x, pos, k):
    vals, idxs = [], []
    for _ in range(k):
        m = jnp.max(x, axis=0, keepdims=True)
        i = jnp.min(jnp.where(x == m, pos, jnp.float32(1e9)), axis=0, keepdims=True)
        vals.append(m)
        idxs.append(i)
        x = jnp.where(pos == i, -jnp.inf, x)
    return vals, idxs


def _head_topk(s1, s2, keys, topk):
    tb = s1.shape[1]
    key_pos = lax.broadcasted_iota(jnp.int32, (keys, tb), 0).astype(jnp.float32)
    nrow = -(-len(PEER_PAIRS) // SUBLANES) * SUBLANES
    pair_pos = lax.broadcasted_iota(jnp.int32, (nrow, tb), 0).astype(jnp.float32)
    v1, i1 = _top_rows(s1, key_pos, topk)
    v2, i2 = _top_rows(s2, key_pos, topk)
    cand = jnp.full((nrow, tb), -jnp.inf, jnp.float32)
    cid = jnp.zeros((nrow, tb), jnp.float32)
    for p, (a, b) in enumerate(PEER_PAIRS):
        cand = jnp.where(pair_pos == p, v1[a] + v2[b], cand)
        cid = jnp.where(pair_pos == p, i1[a] * keys + i2[b], cid)
    top_s, top_p = _top_rows(cand, pair_pos, topk)
    es = [jnp.exp(s - top_s[0]) for s in top_s]
    den = es[0]
    for e in es[1:]:
        den = den + e
    eids = [jnp.sum(jnp.where(pair_pos == top_p[r], cid, 0.0), axis=0, keepdims=True) for r in range(topk)]
    return (jnp.concatenate(eids, axis=0).astype(jnp.int32), jnp.concatenate([e / den for e in es], axis=0))


def _route_head(hb_ref, wq_ref, sk_ref, eid_ref, gate_ref, head, *, keys, topk):
    q = jnp.dot(hb_ref[...].astype(jnp.bfloat16), wq_ref[head], preferred_element_type=jnp.float32)
    half = sk_ref.shape[3]
    nt = (((1,), (1,)), ((), ()))
    s = [lax.dot_general(sk_ref[head, p], q[:, p * half:(p + 1) * half].astype(jnp.bfloat16), nt,
                         preferred_element_type=jnp.float32) for p in range(2)]
    eid, gate = _head_topk(s[0], s[1], keys, topk)
    rows = pl.ds(pl.multiple_of(head * topk, topk), topk)
    eid_ref[rows, :] = eid
    gate_ref[rows, :] = gate


def _route_kernel(hb_ref, wq_ref, sk_ref, eid_ref, gate_ref, *, keys, topk):
    for head in range(wq_ref.shape[0]):
        _route_head(hb_ref, wq_ref, sk_ref, eid_ref, gate_ref, head, keys=keys, topk=topk)


def _route_specs(wq, sk, tb, block_of_step):
    heads = wq.shape[0]
    const = lambda a: pl.BlockSpec(a.shape, lambda i: (0,) * a.ndim)
    ins = [pl.BlockSpec((tb, wq.shape[1]), lambda i: (block_of_step(i), 0)), const(wq), const(sk)]
    outs = [pl.BlockSpec((heads * PEER_TOPK, tb), lambda i: (0, block_of_step(i)))] * 2
    return ins, outs


def _route_rows(eid_t, gate_t):
    n_tok = eid_t.shape[1]
    groups = eid_t.shape[0] // SUBLANES
    gate_tile = jnp.transpose(gate_t.reshape(groups, SUBLANES, n_tok), (2, 1, 0))
    return eid_t.T, jnp.pad(gate_tile, ((0, 0), (0, 0), (0, LANES - groups)))


PEER_STAGES = 10


def _peer_ffn(h, w_pq, sub_keys, expert_u, expert_v, *, tb=LANES):
    N, D = h.shape
    heads, _, keys, half = sub_keys.shape
    E = expert_u.shape[0]
    R = heads * PEER_TOPK
    chunks = D // LANES
    step_tok = 2 * PEER_TOK
    part = N // PEER_STAGES
    assert N % PEER_STAGES == 0 and part % tb == 0 and tb == heads * step_tok and D % LANES == 0
    assert E == keys * keys
    wq = jnp.transpose(w_pq.reshape(D, heads, 2 * half), (1, 0, 2)).astype(jnp.bfloat16)
    sk = sub_keys.astype(jnp.bfloat16)

    def bf16_bits(x):
        return lax.bitcast_convert_type(x.astype(jnp.bfloat16), jnp.uint16).astype(jnp.uint32)
    tab = ((bf16_bits(expert_u) << 16) | bf16_bits(expert_v)).reshape(E * chunks, LANES)
    route_shapes = [jax.ShapeDtypeStruct((R, part), jnp.int32), jax.ShapeDtypeStruct((R, part), jnp.float32)]
    parts = [h[g * part:(g + 1) * part] for g in range(PEER_STAGES)]
    ins, outs = _route_specs(wq, sk, tb, lambda i: i)
    routed = pl.pallas_call(
        functools.partial(_route_kernel, keys=keys, topk=PEER_TOPK),
        grid=(part // tb,), in_specs=ins, out_specs=outs, out_shape=route_shapes,
        compiler_params=pltpu.CompilerParams(dimension_semantics=("parallel",), vmem_limit_bytes=48 * 1024 * 1024),
        name="peer_route",
    )(parts[0], wq, sk)
    nsteps = part // step_tok
    fetch_in = [
        pl.BlockSpec((step_tok, R), lambda i: (i, 0), memory_space=pltpu.SMEM),
        pl.BlockSpec((step_tok, R), lambda i: (jnp.minimum(i + 1, nsteps - 1), 0), memory_space=pltpu.SMEM),
        pl.BlockSpec((step_tok, D), lambda i: (i, 0)),
        pl.BlockSpec((step_tok, SUBLANES, LANES), lambda i: (i, 0, 0)),
        pl.BlockSpec(memory_space=pl.ANY),
    ]
    fetch_out = pl.BlockSpec((step_tok, D), lambda i: (i, 0))
    scratch = [pltpu.VMEM((2 * PEER_TOK * R * (chunks + 1), LANES), jnp.uint32), pltpu.SemaphoreType.DMA((2,))]
    params = pltpu.CompilerParams(dimension_semantics=("arbitrary",), vmem_limit_bytes=56 * 1024 * 1024,
                                  disable_bounds_checks=True)
    ffn = []
    for g in range(PEER_STAGES):
        eid, gate_tile = _route_rows(*routed)
        if g + 1 < PEER_STAGES:
            ins, outs = _route_specs(wq, sk, tb, lambda i: i // heads)
            res = pl.pallas_call(
                functools.partial(_peer_expert_kernel, rows=R, dim=D, route_keys=keys),
                grid=(nsteps,), in_specs=fetch_in + ins, out_specs=[fetch_out] + outs,
                out_shape=[jax.ShapeDtypeStruct((part, D), jnp.float32)] + route_shapes,
                scratch_shapes=scratch, compiler_params=params, name="peer_experts_route",
            )(eid, eid, parts[g], gate_tile, tab, parts[g + 1], wq, sk)
            ffn.append(res[0])
            routed = res[1:]
        else:
            ffn.append(pl.pallas_call(
                functools.partial(_peer_expert_kernel, rows=R, dim=D, route_keys=0),
                grid=(nsteps,), in_specs=fetch_in, out_specs=fetch_out,
                out_shape=jax.ShapeDtypeStruct((part, D), jnp.float32),
                scratch_shapes=scratch, compiler_params=params, name="peer_experts",
            )(eid, eid, parts[g], gate_tile, tab))
    return jnp.concatenate(ffn, axis=0)


def _mix_proj_kernel(oa_ref, yr_ref, x_ref, g1_ref, sc_ref, sh_ref, gn_ref, wa_ref, wr_ref, x1_ref, h2_ref):
    mix = (jnp.dot(oa_ref[0].astype(jnp.bfloat16), wa_ref[...], preferred_element_type=jnp.float32)
           + jnp.dot(yr_ref[0].astype(jnp.bfloat16), wr_ref[...], preferred_element_type=jnp.float32))
    x1 = x_ref[0] + g1_ref[0] * mix
    x1_ref[0] = x1
    h = x1 * lax.rsqrt(jnp.mean(x1 * x1, axis=-1, keepdims=True) + NORM_EPS) * gn_ref[...]
    h2_ref[0] = h * (1.0 + sc_ref[0]) + sh_ref[0]


def _mix_proj(o_attn, y_rwkv, x, g1, sc2, sh2, gain2, w_out):
    B, T, D = x.shape
    ca, cr = o_attn.shape[2], y_rwkv.shape[2]
    tm = min(T, 512)
    assert T % tm == 0
    blk = lambda w: pl.BlockSpec((1, tm, w), lambda b, t: (b, t, 0))
    mod = _mod_spec(g1, tm)
    const = lambda a: pl.BlockSpec(a.shape, lambda b, t: (0,) * a.ndim)
    wa, wr = w_out[:ca].astype(jnp.bfloat16), w_out[ca:].astype(jnp.bfloat16)
    gn = gain2.reshape(1, D)
    return pl.pallas_call(
        _mix_proj_kernel,
        grid=(B, T // tm),
        in_specs=[blk(ca), blk(cr), blk(D), mod, mod, mod, const(gn), const(wa), const(wr)],
        out_specs=[blk(D), blk(D)],
        out_shape=[jax.ShapeDtypeStruct((B, T, D), jnp.float32)] * 2,
        compiler_params=pltpu.CompilerParams(dimension_semantics=("parallel", "parallel"),
                                             vmem_limit_bytes=48 * 1024 * 1024),
        name="mix_proj",
    )(o_attn, y_rwkv, x, g1, sc2, sh2, gn, wa, wr)


def _resid_kernel(x_ref, g_ref, f_ref, o_ref):
    o_ref[...] = x_ref[...] + g_ref[...] * f_ref[...]


def _resid(x, g, f):
    B, T, D = x.shape
    tb = min(T, 512)
    return pl.pallas_call(
        _resid_kernel,
        grid=(B, T // tb),
        in_specs=[pl.BlockSpec((1, tb, D), lambda b, t: (b, t, 0)),
                  pl.BlockSpec((1, 1, D), lambda b, t: (b, 0, 0)),
                  pl.BlockSpec((1, tb, D), lambda b, t: (b, t, 0))],
        out_specs=pl.BlockSpec((1, tb, D), lambda b, t: (b, t, 0)),
        out_shape=jax.ShapeDtypeStruct(x.shape, x.dtype),
        name="ffn_residual",
    )(x, g, f)


def _layer(x, c, k_buf, v_buf, wkv0, shift0, rel_bias, p):
    B, T, _ = x.shape
    mod = jax.nn.silu(c.astype(jnp.float32)) @ p['ada_w'] + p['ada_b']
    sh1, sc1, g1, sh2, sc2, g2 = jnp.split(mod[:, None, :], 6, axis=-1)
    if T < SUBLANES:
        fold = lambda a: a.reshape(1, B * T, a.shape[-1])
        per_token = lambda m: fold(jnp.broadcast_to(m, (B, T, D_MODEL)))
    else:
        fold = per_token = lambda a: a
    unfold = lambda a: a.reshape(B, T, a.shape[-1])
    q3, k3, v3, proj_rwkv = (unfold(a) for a in _in_proj(
        fold(x), p['norm1_g'], per_token(sc1), per_token(sh1), p['w_in'], p['q_norm_g'], p['k_norm_g']))
    q, k, v = (a.reshape(B, T, N_HEADS_ATTN, HEAD_DIM) for a in (q3, k3, v3))
    if k_buf is None:
        o_attn = _prompt_attn(q3, k3, v3, rel_bias)
        keep = min(MAX_WINDOW, T)
        k_rows, v_rows = k[:, T - keep:], v[:, T - keep:]
        wkv0 = jnp.zeros((B, N_HEADS_RWKV, HEAD_DIM, HEAD_DIM), jnp.float32)
        shift0 = jnp.zeros((B, COLS_RWKV), jnp.float32)
    else:
        o_attn = _sample_attn(q, k, v, jnp.transpose(k_buf, (0, 2, 3, 1)), jnp.transpose(v_buf, (0, 2, 3, 1)), rel_bias)
        k_rows, v_rows = k, v
    y_rwkv, s_fin, shift_new = _rwkv_mixer(
        proj_rwkv, shift0, wkv0, p['mu_shift'], p['w0'], p['w_w2'], p['a0'], p['w_a2'],
        p['w_g2'], p['k_k'], p['k_a'], p['r_k'], p['lnx_g'], p['lnx_b'])
    x, h2 = (unfold(a) for a in _mix_proj(
        fold(o_attn.reshape(B, T, C_ATTN)), fold(y_rwkv), fold(x), per_token(g1), per_token(sc2), per_token(sh2),
        p['norm2_g'], p['w_out']))
    return x, h2, g2, k_rows, v_rows, s_fin, shift_new


def kernel(x_prompt, x_sample, c_prompt, c_sample, cache_k_win, cache_v_win, state_wkv, state_shift,
           ada_w, ada_b, norm1_g, norm2_g, w_in, q_norm_g, k_norm_g, rel_bias, mu_shift, w0, w_w2, a0,
           w_a2, w_g2, k_k, k_a, r_k, lnx_g, lnx_b, w_out, w_peer_q, peer_sub_keys, expert_u, expert_v):
    xp, xs = x_prompt, x_sample
    kp_l, vp_l, sp_l, hp_l = [], [], [], []
    ks_l, vs_l, ss_l, hs_l = [], [], [], []
    names = ('ada_w', 'ada_b', 'norm1_g', 'norm2_g', 'w_in', 'q_norm_g', 'k_norm_g', 'mu_shift', 'w0', 'w_w2',
             'a0', 'w_a2', 'w_g2', 'k_k', 'k_a', 'r_k', 'lnx_g', 'lnx_b', 'w_out', 'w_peer_q', 'peer_sub_keys',
             'expert_u', 'expert_v')
    vals = (ada_w, ada_b, norm1_g, norm2_g, w_in, q_norm_g, k_norm_g, mu_shift, w0, w_w2, a0, w_a2, w_g2, k_k,
            k_a, r_k, lnx_g, lnx_b, w_out, w_peer_q, peer_sub_keys, expert_u, expert_v)
    for l in range(DEPTH):
        p = {n: v[l] for n, v in zip(names, vals)}
        xp, h2p, g2p, kp, vp, sp, hp = _layer(xp, c_prompt, None, None, None, None, rel_bias, p)
        xs, h2s, g2s, kn, vn, sn, hn = _layer(xs, c_sample, cache_k_win[l], cache_v_win[l], state_wkv[l],
                                              state_shift[l], rel_bias, p)
        n_p = h2p.shape[0] * h2p.shape[1]
        h2 = jnp.concatenate([h2p.reshape(-1, D_MODEL), h2s.reshape(-1, D_MODEL)], axis=0)
        ffn = _peer_ffn(h2, p['w_peer_q'], p['peer_sub_keys'], p['expert_u'], p['expert_v'])
        xp = _resid(xp, g2p, ffn[:n_p].reshape(xp.shape))
        xs = _resid(xs, g2s, ffn[n_p:].reshape(xs.shape))
        kp_l.append(kp); vp_l.append(vp); sp_l.append(sp); hp_l.append(hp)
        ks_l.append(kn); vs_l.append(vn); ss_l.append(sn); hs_l.append(hn)
    return (xp, xs, jnp.stack(kp_l), jnp.stack(vp_l), jnp.stack(sp_l), jnp.stack(hp_l),
            jnp.stack(ks_l), jnp.stack(vs_l), jnp.stack(ss_l), jnp.stack(hs_l))
```
